```python
import jax, jax.numpy as jnp
from jax import lax
import numpy as np

D_MODEL = 1024
BATCH = 2
SEQ = 8192
DEPTH = 1

GRID_W = 64
HG_HEADS = 4
HG_DK = 128
HG_DV = 128
HG_W = HG_HEADS * HG_DK
HG_CHUNK = 64
AT_HEADS = 8
AT_KV_HEADS = 2
AT_GROUP = AT_HEADS // AT_KV_HEADS
AT_HD = 64
AT_W = AT_HEADS * AT_HD
AT_KV_W = AT_KV_HEADS * AT_HD
Q_BLOCK = 128
ROPE_THETA = 10000.0
ROPE_AXIS_DIM = AT_HD // 2
D_FF = 2816
CONV_W = 3
EPS = 1e-6

IN_SIZES = (HG_W, HG_W, HG_W, HG_W, HG_W,
            AT_W, AT_KV_W, AT_KV_W,
            D_MODEL, D_MODEL)
D_IN = int(sum(IN_SIZES))
IN_SPLITS = tuple(int(v) for v in np.cumsum(IN_SIZES)[:-1])

kernel_name = "hybrid_hgrn2_gqa2drope_convffn_encoder"


def rmsnorm(x, g):
    xf = x.astype(jnp.float32)
    return xf * lax.rsqrt(jnp.mean(xf * xf, axis=-1, keepdims=True) + EPS) * g.astype(jnp.float32)


def rotate_half(x):
    x1, x2 = jnp.split(x, 2, axis=-1)
    return jnp.concatenate([-x2, x1], axis=-1)


def axial_rope_tables(L):
    rows = L // GRID_W
    pos_row = jnp.broadcast_to(jnp.arange(rows, dtype=jnp.float32)[:, None], (rows, GRID_W)).reshape(L)
    pos_col = jnp.broadcast_to(jnp.arange(GRID_W, dtype=jnp.float32)[None, :], (rows, GRID_W)).reshape(L)
    inv = ROPE_THETA ** (-jnp.arange(0, ROPE_AXIS_DIM, 2, dtype=jnp.float32) / ROPE_AXIS_DIM)
    def tab(pos):
        ang = pos[:, None] * inv[None, :]
        ang = jnp.concatenate([ang, ang], axis=-1)
        return jnp.cos(ang)[None, :, None, :], jnp.sin(ang)[None, :, None, :]
    return tab(pos_row), tab(pos_col)


def apply_axial_rope(x, tabs):
    (cr, sr), (cc, sc) = tabs
    xr, xc = x[..., :ROPE_AXIS_DIM], x[..., ROPE_AXIS_DIM:]
    xr = xr * cr + rotate_half(xr) * sr
    xc = xc * cc + rotate_half(xc) * sc
    return jnp.concatenate([xr, xc], axis=-1)


def gla_chunkwise(q, k, v, g):
    B, H, L, dk = q.shape
    dv = v.shape[-1]
    C = HG_CHUNK
    N = L // C
    q = q.reshape(B, H, N, C, dk)
    k = k.reshape(B, H, N, C, dk)
    v = v.reshape(B, H, N, C, dv)
    b = jnp.cumsum(g.reshape(B, H, N, C, dk), axis=3)
    b_last = b[:, :, :, -1:, :]
    qd = q * jnp.exp(b)
    kd = k * jnp.exp(-b)
    A = jnp.einsum('bhncd,bhnsd->bhncs', qd, kd)
    mask = jnp.tril(jnp.ones((C, C), dtype=bool))
    A = jnp.where(mask, A, 0.0)
    intra = jnp.einsum('bhncs,bhnse->bhnce', A, v)
    kc = k * jnp.exp(b_last - b)
    U = jnp.einsum('bhnsd,bhnse->bhnde', kc, v)
    decay = jnp.exp(b_last[:, :, :, 0, :])

    def step(S, inp):
        dec, u = inp
        return dec[..., None] * S + u, S

    S0 = jnp.zeros((B, H, dk, dv), jnp.float32)
    _, S_prev = lax.scan(step, S0, (jnp.moveaxis(decay, 2, 0), jnp.moveaxis(U, 2, 0)))
    S_prev = jnp.moveaxis(S_prev, 0, 2)
    inter = jnp.einsum('bhncd,bhnde->bhnce', qd, S_prev)
    return (intra + inter).reshape(B, H, L, dv)


def hgrn2_bidirectional(hq, hi, hf_fwd, hf_bwd, hgate, lb_fwd, lb_bwd, onorm_g):
    B, L, _ = hq.shape
    def heads(t):
        return t.reshape(B, L, HG_HEADS, -1).transpose(0, 2, 1, 3)
    q = heads(jax.nn.silu(hq))
    v = heads(hi)

    def direction(fpre, lb):
        f = lb + (1.0 - lb) * jax.nn.sigmoid(fpre)
        return heads(1.0 - f), heads(jnp.log(f))

    k_f, g_f = direction(hf_fwd, lb_fwd)
    k_b, g_b = direction(hf_bwd, lb_bwd)
    o_f = gla_chunkwise(q, k_f, v, g_f)
    flip = lambda t: jnp.flip(t, axis=2)
    o_b = flip(gla_chunkwise(flip(q), flip(k_b), flip(v), flip(g_b)))
    o = (o_f + o_b).transpose(0, 2, 1, 3)
    o = rmsnorm(o, onorm_g) * jax.nn.silu(hgate.reshape(B, L, HG_HEADS, HG_DV))
    return o.reshape(B, L, HG_HEADS * HG_DV)


def gqa_bidirectional(aq, ak, av, q_norm_g, k_norm_g):
    B, L, _ = aq.shape
    tabs = axial_rope_tables(L)
    q = rmsnorm(aq.reshape(B, L, AT_HEADS, AT_HD), q_norm_g)
    k = rmsnorm(ak.reshape(B, L, AT_KV_HEADS, AT_HD), k_norm_g)
    v = av.reshape(B, L, AT_KV_HEADS, AT_HD).astype(jnp.float32)
    q = apply_axial_rope(q, tabs) * (AT_HD ** -0.5)
    k = apply_axial_rope(k, tabs)
    nb = L // Q_BLOCK
    qb = q.reshape(B, nb, Q_BLOCK, AT_KV_HEADS, AT_GROUP, AT_HD)
    qb = jnp.moveaxis(qb, 1, 0)

    def block(qblk):
        s = jnp.einsum('bqkgd,bskd->bkgqs', qblk, k)
        p = jax.nn.softmax(s, axis=-1)
        return jnp.einsum('bkgqs,bskd->bqkgd', p, v)

    o = lax.map(block, qb)
    o = jnp.moveaxis(o, 0, 1).reshape(B, L, AT_W)
    return o


def dwconv_centred(x, w, b):
    xp = jnp.pad(x, ((0, 0), (1, 1), (0, 0)))
    return xp[:, :-2] * w[0] + xp[:, 1:-1] * w[1] + xp[:, 2:] * w[2] + b


def setup_inputs(seed: int = 0) -> dict:
    key = jax.random.key(seed)
    ks = jax.random.split(key, 16)
    f32 = jnp.float32
    def nrm(k, shape, scale):
        return jax.random.normal(k, shape, f32) * scale
    return {
        "x": jax.random.normal(ks[0], (BATCH, SEQ, D_MODEL), f32),
        "norm1_g": 1.0 + nrm(ks[1], (DEPTH, D_MODEL), 0.02),
        "w_in": nrm(ks[2], (DEPTH, D_MODEL, D_IN), D_MODEL ** -0.5),
        "hg_lb_fwd": nrm(ks[3], (DEPTH + 1, HG_W), 0.1),
        "hg_lb_bwd": nrm(ks[4], (DEPTH + 1, HG_W), 0.1),
        "hg_onorm_g": 1.0 + nrm(ks[5], (DEPTH, HG_DV), 0.02),
        "q_norm_g": 1.0 + nrm(ks[6], (DEPTH, AT_HD), 0.02),
        "k_norm_g": 1.0 + nrm(ks[7], (DEPTH, AT_HD), 0.02),
        "w_branch_a": nrm(ks[8], (DEPTH, HG_HEADS * HG_DV, D_MODEL), (HG_HEADS * HG_DV) ** -0.5),
        "w_branch_b": nrm(ks[9], (DEPTH, AT_W, D_MODEL), AT_W ** -0.5),
        "w_out": nrm(ks[10], (DEPTH, D_MODEL, D_MODEL), 0.5 * D_MODEL ** -0.5),
        "norm2_g": 1.0 + nrm(ks[11], (DEPTH, D_MODEL), 0.02),
        "w_up": nrm(ks[12], (DEPTH, D_MODEL, 2 * D_FF), D_MODEL ** -0.5),
        "conv_w": nrm(ks[13], (DEPTH, CONV_W, 2 * D_FF), CONV_W ** -0.5),
        "conv_b": nrm(ks[14], (DEPTH, 2 * D_FF), 0.01),
        "w_down": nrm(ks[15], (DEPTH, D_FF, D_MODEL), 0.5 * D_FF ** -0.5),
    }


def reference(x, norm1_g, w_in, hg_lb_fwd, hg_lb_bwd, hg_onorm_g, q_norm_g, k_norm_g,
              w_branch_a, w_branch_b, w_out, norm2_g, w_up, conv_w, conv_b, w_down):
    dt = x.dtype
    lb_tab_f = jnp.cumsum(jax.nn.softmax(hg_lb_fwd.astype(jnp.float32), axis=0), axis=0)
    lb_tab_b = jnp.cumsum(jax.nn.softmax(hg_lb_bwd.astype(jnp.float32), axis=0), axis=0)
    for l in range(DEPTH):
        u = rmsnorm(x, norm1_g[l])
        proj = u @ w_in[l].astype(jnp.float32)
        hq, hi, hf_f, hf_b, hgate, aq, ak, av, ga, gb = jnp.split(proj, IN_SPLITS, axis=-1)
        o_a = hgrn2_bidirectional(hq, hi, hf_f, hf_b, hgate, lb_tab_f[l], lb_tab_b[l], hg_onorm_g[l])
        o_b = gqa_bidirectional(aq, ak, av, q_norm_g[l], k_norm_g[l])
        y_a = o_a @ w_branch_a[l].astype(jnp.float32)
        y_b = o_b @ w_branch_b[l].astype(jnp.float32)
        merged = jax.nn.sigmoid(ga) * y_a + jax.nn.sigmoid(gb) * y_b
        x = (x.astype(jnp.float32) + merged @ w_out[l].astype(jnp.float32)).astype(dt)
        hn = rmsnorm(x, norm2_g[l])
        up = hn @ w_up[l].astype(jnp.float32)
        up = dwconv_centred(up, conv_w[l].astype(jnp.float32), conv_b[l].astype(jnp.float32))
        val, gate = jnp.split(up, 2, axis=-1)
        ff = (jax.nn.silu(gate) * val) @ w_down[l].astype(jnp.float32)
        x = (x.astype(jnp.float32) + ff).astype(dt)
    return x
```

```python
import functools

import jax
import jax.numpy as jnp
import numpy as np
from jax import lax
from jax.experimental import pallas as pl
from jax.experimental.pallas import tpu as pltpu

F32 = jnp.float32
BF16 = jnp.bfloat16

D_MODEL = 1024
GRID_W = 64
HG_HEADS = 4
HG_DK = 128
HG_W = HG_HEADS * HG_DK
HG_CHUNK = 64
AT_HEADS = 8
AT_KV_HEADS = 2
AT_GROUP = AT_HEADS // AT_KV_HEADS
AT_HD = 64
AT_W = AT_HEADS * AT_HD
AT_KV_W = AT_KV_HEADS * AT_HD
ROPE_THETA = 10000.0
ROPE_AXIS_DIM = AT_HD // 2
D_FF = 2816
EPS = 1e-6

LANES = 128
BF16_SUBLANES = 16
VMEM_LIMIT = 56 * 1024 * 1024

VT_ROWS = AT_HD + BF16_SUBLANES
HALO = BF16_SUBLANES

TM_PROJ = 512
HG_SUB = 4 * HG_CHUNK
HG_ROWS = 512
TQ = 512
TK = 512
TM_MERGE = 512
TM_FFN = 512
FF_CHUNK = 256


def _dot(a, b):
    return jnp.dot(a, b, preferred_element_type=F32)


def _dot_nt(a, b):
    return lax.dot_general(a, b, (((1,), (1,)), ((), ())), preferred_element_type=F32)


def _dot_tn(a, b):
    return lax.dot_general(a, b, (((0,), (0,)), ((), ())), preferred_element_type=F32)


def _sigmoid(x):
    return 1.0 / (1.0 + jnp.exp(-x))


def _rms_rows(x, g):
    ms = jnp.mean(x * x, axis=-1, keepdims=True)
    return x * lax.rsqrt(ms + EPS) * g


def _const_spec(shape):
    nd = len(shape)
    return pl.BlockSpec(shape, lambda *_: (0,) * nd)


def _lower_bound(lb_ref):
    a0 = lb_ref[0:1, :]
    a1 = lb_ref[1:2, :]
    m = jnp.maximum(a0, a1)
    e0 = jnp.exp(a0 - m)
    e1 = jnp.exp(a1 - m)
    return e0 / (e0 + e1)


def _head_rms(t, gsum):
    sq = t * t
    hi = sq.astype(BF16)
    lo = (sq - hi.astype(F32)).astype(BF16)
    ms = _dot(hi, gsum) + _dot(lo, gsum)
    return t * lax.rsqrt(ms + EPS)


def _rope(t, cos, sin_signed, first_half):
    from_hi = pltpu.roll(t, LANES - ROPE_AXIS_DIM // 2, axis=1)
    from_lo = pltpu.roll(t, ROPE_AXIS_DIM // 2, axis=1)
    return t * cos + jnp.where(first_half, from_hi, from_lo) * sin_signed


def _inproj_kernel(x_ref, n1_ref, whg_ref, wat_ref, lbf_ref, lbb_ref, qg_ref, kg_ref,
                   cos_ref, sin_ref, gsum_ref,
                   hq_ref, hv_ref, gf_ref, gb_ref, aq_ref, ak_ref, vt_ref):
    x = x_ref[0]
    u = _rms_rows(x, n1_ref[...]).astype(BF16)

    hq = _dot(u, whg_ref[:, 0:HG_W])
    hq_ref[0] = hq * _sigmoid(hq)
    hv_ref[0] = _dot(u, whg_ref[:, HG_W:2 * HG_W]).astype(BF16)
    for out_ref, lb_ref, off in ((gf_ref, lbf_ref, 2 * HG_W), (gb_ref, lbb_ref, 3 * HG_W)):
        pre = _dot(u, whg_ref[:, off:off + HG_W])
        lb = _lower_bound(lb_ref)
        out_ref[0] = jnp.log(lb + (1.0 - lb) * _sigmoid(pre))

    cos = cos_ref[...]
    sin = sin_ref[...]
    lane = lax.broadcasted_iota(jnp.int32, cos.shape, 1)
    first_half = (lane & (ROPE_AXIS_DIM - 1)) < (ROPE_AXIS_DIM // 2)
    sin_signed = jnp.where(first_half, -sin, sin)
    gsum = gsum_ref[...]

    aq = _dot(u, wat_ref[:, 0:AT_W])
    qn = _head_rms(aq, gsum)
    scale = AT_HD ** -0.5
    for c in range(AT_W // LANES):
        sl = slice(c * LANES, (c + 1) * LANES)
        t = qn[:, sl] * qg_ref[...]
        aq_ref[0, :, sl] = (_rope(t, cos, sin_signed, first_half) * scale).astype(BF16)

    ak = _dot(u, wat_ref[:, AT_W:AT_W + AT_KV_W])
    kn = _head_rms(ak, gsum[0:LANES, 0:LANES]) * kg_ref[...]
    kr = _rope(kn, cos, sin_signed, first_half).astype(BF16)
    for g in range(AT_KV_HEADS):
        ak_ref[0, g] = kr[:, g * AT_HD:(g + 1) * AT_HD]

    av = _dot(u, wat_ref[:, AT_W + AT_KV_W:AT_W + 2 * AT_KV_W])
    avt = av.T
    tm = x.shape[0]
    row = lax.broadcasted_iota(jnp.int32, (VT_ROWS - AT_HD, tm), 0)
    aug = jnp.where(row == 0, 1.0, 0.0).astype(BF16)
    for g in range(AT_KV_HEADS):
        vt_ref[0, g, 0:AT_HD, :] = avt[g * AT_HD:(g + 1) * AT_HD].astype(BF16)
        vt_ref[0, g, AT_HD:VT_ROWS, :] = aug


def _inproj(x, n1, whg, wat, lbf, lbb, qg2, kg2, cos, sin, gsum):
    B, L, D = x.shape
    tm = min(TM_PROJ, L)
    nt = L // tm
    tok = lambda w: pl.BlockSpec((1, tm, w), lambda b, i: (b, i, 0))
    out_shape = (
        jax.ShapeDtypeStruct((B, L, HG_W), F32),
        jax.ShapeDtypeStruct((B, L, HG_W), BF16),
        jax.ShapeDtypeStruct((B, L, HG_W), F32),
        jax.ShapeDtypeStruct((B, L, HG_W), F32),
        jax.ShapeDtypeStruct((B, L, AT_W), BF16),
        jax.ShapeDtypeStruct((B, AT_KV_HEADS, L, AT_HD), BF16),
        jax.ShapeDtypeStruct((B, AT_KV_HEADS, VT_ROWS, L), BF16),
    )
    out_specs = (
        tok(HG_W), tok(HG_W), tok(HG_W), tok(HG_W), tok(AT_W),
        pl.BlockSpec((1, AT_KV_HEADS, tm, AT_HD), lambda b, i: (b, 0, i, 0)),
        pl.BlockSpec((1, AT_KV_HEADS, VT_ROWS, tm), lambda b, i: (b, 0, 0, i)),
    )
    in_specs = [
        tok(D), _const_spec(n1.shape), _const_spec(whg.shape), _const_spec(wat.shape),
        _const_spec(lbf.shape), _const_spec(lbb.shape), _const_spec(qg2.shape), _const_spec(kg2.shape),
        pl.BlockSpec((tm, LANES), lambda b, i: (i, 0)),
        pl.BlockSpec((tm, LANES), lambda b, i: (i, 0)),
        _const_spec(gsum.shape),
    ]
    return pl.pallas_call(
        _inproj_kernel,
        grid=(B, nt),
        in_specs=in_specs,
        out_specs=out_specs,
        out_shape=out_shape,
        compiler_params=pltpu.CompilerParams(
            dimension_semantics=("arbitrary", "arbitrary"), vmem_limit_bytes=VMEM_LIMIT),
        name="inproj",
    )(x, n1, whg, wat, lbf, lbb, qg2, kg2, cos, sin, gsum)


def _hgrn_sub(q, v, g, st, reverse):
    n = q.shape[0]
    nchunk = n // HG_CHUNK
    shift = HG_CHUNK.bit_length() - 1
    r = lax.broadcasted_iota(jnp.int32, (n, n), 0)
    c = lax.broadcasted_iota(jnp.int32, (n, n), 1)
    same = lax.shift_right_logical(r, shift) == lax.shift_right_logical(c, shift)
    tri = same & ((c >= r) if reverse else (c <= r))
    trib = jnp.where(tri, 1.0, 0.0).astype(BF16)

    g1 = g.astype(BF16)
    rem = g - g1.astype(F32)
    g2 = rem.astype(BF16)
    g3 = (rem - g2.astype(F32)).astype(BF16)
    b = _dot(trib, g1) + _dot(trib, g2) + _dot(trib, g3)
    b3 = b.reshape(nchunk, HG_CHUNK, HG_DK)
    tot = b3[:, 0:1, :] if reverse else b3[:, HG_CHUNK - 1:HG_CHUNK, :]
    totf = jnp.broadcast_to(tot, b3.shape).reshape(n, HG_DK)

    k = 1.0 - jnp.exp(g)
    qd = (q * jnp.exp(b)).astype(BF16)
    kd = (k * jnp.exp(-b)).astype(BF16)
    kc = (k * jnp.exp(totf - b)).astype(BF16)
    dec = jnp.exp(tot)

    a = jnp.where(tri, _dot_nt(qd, kd), 0.0).astype(BF16)
    intra = _dot(a, v)

    outs = [None] * nchunk
    order = range(nchunk - 1, -1, -1) if reverse else range(nchunk)
    for ci in order:
        sl = slice(ci * HG_CHUNK, (ci + 1) * HG_CHUNK)
        outs[ci] = intra[sl] + _dot_nt(qd[sl], st.astype(BF16))
        st = st * dec[ci] + _dot_tn(v[sl], kc[sl])
    return jnp.concatenate(outs, axis=0), st


def _hgrn_dir(q_ref, v_ref, g_ref, o_ref, st_ref, reverse):
    rows = q_ref.shape[1]
    nsub = rows // HG_SUB
    st = st_ref[...]
    order = range(nsub - 1, -1, -1) if reverse else range(nsub)
    for s in order:
        sl = slice(s * HG_SUB, (s + 1) * HG_SUB)
        o, st = _hgrn_sub(q_ref[0, sl, :], v_ref[0, sl, :], g_ref[0, sl, :], st, reverse)
        o_ref[0, sl, :] = o
    st_ref[...] = st


def _hgrn_kernel(qf_ref, vf_ref, gf_ref, qb_ref, vb_ref, gb_ref, of_ref, ob_ref, stf_ref, stb_ref):
    @pl.when(pl.program_id(2) == 0)
    def _():
        stf_ref[...] = jnp.zeros_like(stf_ref)
        stb_ref[...] = jnp.zeros_like(stb_ref)

    _hgrn_dir(qf_ref, vf_ref, gf_ref, of_ref, stf_ref, False)
    _hgrn_dir(qb_ref, vb_ref, gb_ref, ob_ref, stb_ref, True)


def _hgrn(hq, hv, gf, gb):
    B, L, _ = hq.shape
    rows = min(HG_ROWS, L)
    nb = L // rows
    fwd = pl.BlockSpec((1, rows, HG_DK), lambda b, h, i: (b, i, h))
    bwd = pl.BlockSpec((1, rows, HG_DK), lambda b, h, i: (b, nb - 1 - i, h))
    o = jax.ShapeDtypeStruct((B, L, HG_W), F32)
    return pl.pallas_call(
        _hgrn_kernel,
        grid=(B, HG_HEADS, nb),
        in_specs=[fwd, fwd, fwd, bwd, bwd, bwd],
        out_specs=(fwd, bwd),
        out_shape=(o, o),
        scratch_shapes=[pltpu.VMEM((HG_DK, HG_DK), F32), pltpu.VMEM((HG_DK, HG_DK), F32)],
        compiler_params=pltpu.CompilerParams(
            dimension_semantics=("arbitrary", "arbitrary", "arbitrary"), vmem_limit_bytes=VMEM_LIMIT),
        name="hgrn2",
    )(hq, hv, gf, hq, hv, gb)


def _attn_kernel(q_ref, k_ref, vt_ref, o_ref, *, tk):
    tq = q_ref.shape[1]
    nk = k_ref.shape[2] // tk
    pairs = []
    for pair in range(AT_GROUP // 2):
        normed = []
        for jj in range(2):
            j = pair * 2 + jj
            qj = q_ref[0, :, j * AT_HD:(j + 1) * AT_HD]

            def body(i, carry, qj=qj):
                m, acc = carry
                start = pl.multiple_of(i * tk, tk)
                kb = k_ref[0, 0, pl.ds(start, tk), :]
                st = _dot_nt(kb, qj)
                m_new = jnp.maximum(m, jnp.max(st, axis=0, keepdims=True))
                alpha = jnp.exp(m - m_new)
                p = jnp.exp(st - m_new).astype(BF16)
                vb = vt_ref[0, 0, :, pl.ds(start, tk)]
                return m_new, alpha * acc + _dot(vb, p)

            m0 = jnp.full((1, tq), -1e30, F32)
            acc0 = jnp.zeros((VT_ROWS, tq), F32)
            _, acc = lax.fori_loop(0, nk, body, (m0, acc0))
            normed.append(acc[0:AT_HD] / acc[AT_HD:AT_HD + 1])
        pairs.append(jnp.concatenate(normed, axis=0).T)
    o_ref[0] = jnp.concatenate(pairs, axis=1).astype(o_ref.dtype)


def _attention(aq, ak, vt):
    B, L, _ = aq.shape
    tq = min(TQ, L)
    tk = min(TK, L)
    gw = AT_GROUP * AT_HD
    return pl.pallas_call(
        functools.partial(_attn_kernel, tk=tk),
        grid=(B, AT_KV_HEADS, L // tq),
        in_specs=[
            pl.BlockSpec((1, tq, gw), lambda b, g, i: (b, i, g)),
            pl.BlockSpec((1, 1, L, AT_HD), lambda b, g, i: (b, g, 0, 0)),
            pl.BlockSpec((1, 1, VT_ROWS, L), lambda b, g, i: (b, g, 0, 0)),
        ],
        out_specs=pl.BlockSpec((1, tq, gw), lambda b, g, i: (b, i, g)),
        out_shape=jax.ShapeDtypeStruct((B, L, AT_W), BF16),
        compiler_params=pltpu.CompilerParams(
            dimension_semantics=("arbitrary", "arbitrary", "arbitrary"), vmem_limit_bytes=VMEM_LIMIT),
        name="attention",
    )(aq, ak, vt)


def _merge_kernel(x_ref, of_ref, ob_ref, oat_ref, n1_ref, wg_ref, ong_ref, wa_ref, wb_ref, wo_ref,
                  n2_ref, h_ref, hn_ref):
    x = x_ref[0]
    u = _rms_rows(x, n1_ref[...]).astype(BF16)
    hgate = _dot(u, wg_ref[:, 0:HG_W])
    o = of_ref[0] + ob_ref[0]
    heads = []
    for h in range(HG_HEADS):
        sl = slice(h * HG_DK, (h + 1) * HG_DK)
        heads.append(_rms_rows(o[:, sl], ong_ref[...]))
    oa = (jnp.concatenate(heads, axis=1) * (hgate * _sigmoid(hgate))).astype(BF16)
    ya = _dot(oa, wa_ref[...])
    yb = _dot(oat_ref[0], wb_ref[...])
    ga = _dot(u, wg_ref[:, HG_W:HG_W + D_MODEL])
    gb = _dot(u, wg_ref[:, HG_W + D_MODEL:HG_W + 2 * D_MODEL])
    merged = (_sigmoid(ga) * ya + _sigmoid(gb) * yb).astype(BF16)
    h = x + _dot(merged, wo_ref[...])
    h_ref[0] = h
    hn_ref[0] = _rms_rows(h, n2_ref[...]).astype(BF16)


def _merge(x, o_f, o_b, o_at, n1, wg, ong, wa, wb, wo, n2):
    B, L, D = x.shape
    tm = min(TM_MERGE, L)
    tok = lambda w: pl.BlockSpec((1, tm, w), lambda b, i: (b, i, 0))
    return pl.pallas_call(
        _merge_kernel,
        grid=(B, L // tm),
        in_specs=[tok(D), tok(HG_W), tok(HG_W), tok(AT_W), _const_spec(n1.shape), _const_spec(wg.shape),
                  _const_spec(ong.shape), _const_spec(wa.shape), _const_spec(wb.shape),
                  _const_spec(wo.shape), _const_spec(n2.shape)],
        out_specs=(tok(D), tok(D)),
        out_shape=(jax.ShapeDtypeStruct((B, L, D), F32), jax.ShapeDtypeStruct((B, L, D), BF16)),
        compiler_params=pltpu.CompilerParams(
            dimension_semantics=("arbitrary", "arbitrary"), vmem_limit_bytes=VMEM_LIMIT),
        name="merge",
    )(x, o_f, o_b, o_at, n1, wg, ong, wa, wb, wo, n2)


def _ffn_kernel(h_ref, hn_ref, prev_ref, next_ref, wup_ref, cw_ref, cb_ref, wdn_ref, y_ref, ext_ref):
    i = pl.program_id(1)
    nt = pl.num_programs(1)
    tm = hn_ref.shape[1]
    ext = tm + 2 * HALO
    ext_ref[0:HALO, :] = jnp.where(i > 0, prev_ref[0], jnp.zeros_like(prev_ref[0]))
    ext_ref[HALO:HALO + tm, :] = hn_ref[0]
    ext_ref[HALO + tm:ext, :] = jnp.where(i < nt - 1, next_ref[0], jnp.zeros_like(next_ref[0]))
    hx = ext_ref[...]

    def conv(t, off):
        w = cw_ref[:, off:off + FF_CHUNK]
        before = pltpu.roll(t, 1, axis=0)[HALO:HALO + tm]
        after = pltpu.roll(t, ext - 1, axis=0)[HALO:HALO + tm]
        return (before * w[0:1] + t[HALO:HALO + tm] * w[1:2] + after * w[2:3]
                + cb_ref[:, off:off + FF_CHUNK])

    acc = jnp.zeros((tm, D_MODEL), F32)
    for j in range(D_FF // FF_CHUNK):
        off = j * FF_CHUNK
        val = conv(_dot(hx, wup_ref[:, off:off + FF_CHUNK]), off)
        gate = conv(_dot(hx, wup_ref[:, D_FF + off:D_FF + off + FF_CHUNK]), D_FF + off)
        act = (gate * _sigmoid(gate) * val).astype(BF16)
        acc = acc + _dot(act, wdn_ref[off:off + FF_CHUNK, :])
    y_ref[0] = h_ref[0] + acc


def _ffn(h, hn, wup, cw, cb, wdn):
    B, L, D = h.shape
    tm = min(TM_FFN, L)
    per = tm // HALO
    nh = L // HALO
    tok = pl.BlockSpec((1, tm, D), lambda b, i: (b, i, 0))
    return pl.pallas_call(
        _ffn_kernel,
        grid=(B, L // tm),
        in_specs=[
            tok, tok,
            pl.BlockSpec((1, HALO, D), lambda b, i: (b, jnp.maximum(i * per - 1, 0), 0)),
            pl.BlockSpec((1, HALO, D), lambda b, i: (b, jnp.minimum((i + 1) * per, nh - 1), 0)),
            _const_spec(wup.shape), _const_spec(cw.shape), _const_spec(cb.shape), _const_spec(wdn.shape),
        ],
        out_specs=tok,
        out_shape=jax.ShapeDtypeStruct((B, L, D), F32),
        scratch_shapes=[pltpu.VMEM((tm + 2 * HALO, D), BF16)],
        compiler_params=pltpu.CompilerParams(
            dimension_semantics=("arbitrary", "arbitrary"), vmem_limit_bytes=VMEM_LIMIT),
        name="convffn",
    )(h, hn, hn, hn, wup, cw, cb, wdn)


def _rope_tables(L):
    pos = jnp.arange(L, dtype=jnp.int32)
    pos_row = (pos // GRID_W).astype(F32)
    pos_col = (pos % GRID_W).astype(F32)
    inv = ROPE_THETA ** (-jnp.arange(0, ROPE_AXIS_DIM, 2, dtype=F32) / ROPE_AXIS_DIM)
    ang_r = pos_row[:, None] * inv[None, :]
    ang_c = pos_col[:, None] * inv[None, :]
    ang = jnp.concatenate([ang_r, ang_r, ang_c, ang_c], axis=-1)
    ang = jnp.concatenate([ang] * (LANES // AT_HD), axis=-1)
    return jnp.cos(ang), jnp.sin(ang)


def _layer(x, norm1_g, w_in, lb_f, lb_b, onorm_g, q_norm_g, k_norm_g, w_a, w_b, w_out, norm2_g,
           w_up, conv_w, conv_b, w_down):
    B, L, D = x.shape
    c_hg = 4 * HG_W
    c_gate = 5 * HG_W
    c_at = c_gate + AT_W + 2 * AT_KV_W
    whg = w_in[:, 0:c_hg].astype(BF16)
    wat = w_in[:, c_gate:c_at].astype(BF16)
    wg = jnp.concatenate([w_in[:, c_hg:c_gate], w_in[:, c_at:]], axis=1).astype(BF16)
    n1 = norm1_g.reshape(1, D)
    n2 = norm2_g.reshape(1, D)
    qg2 = jnp.tile(q_norm_g.reshape(1, AT_HD), (1, LANES // AT_HD))
    kg2 = jnp.tile(k_norm_g.reshape(1, AT_HD), (1, LANES // AT_HD))
    cos, sin = _rope_tables(L)
    blk = np.arange(AT_W) // AT_HD
    gsum = jnp.asarray((blk[:, None] == blk[None, :]).astype(np.float32) / AT_HD, dtype=BF16)

    hq, hv, gf, gb, aq, ak, vt = _inproj(x, n1, whg, wat, lb_f, lb_b, qg2, kg2, cos, sin, gsum)
    o_f, o_b = _hgrn(hq, hv, gf, gb)
    o_at = _attention(aq, ak, vt)
    h, hn = _merge(x, o_f, o_b, o_at, n1, wg, onorm_g.reshape(1, HG_DK), w_a.astype(BF16),
                   w_b.astype(BF16), w_out.astype(BF16), n2)
    return _ffn(h, hn, w_up.astype(BF16), conv_w, conv_b.reshape(1, 2 * D_FF), w_down.astype(BF16))


def kernel(x, norm1_g, w_in, hg_lb_fwd, hg_lb_bwd, hg_onorm_g, q_norm_g, k_norm_g, w_branch_a,
           w_branch_b, w_out, norm2_g, w_up, conv_w, conv_b, w_down):
    depth = w_in.shape[0]
    assert depth == 1 and hg_lb_fwd.shape[0] == 2
    l = 0
    return _layer(x, norm1_g[l], w_in[l], hg_lb_fwd, hg_lb_bwd, hg_onorm_g[l], q_norm_g[l], k_norm_g[l],
                  w_branch_a[l], w_branch_b[l], w_out[l], norm2_g[l], w_up[l], conv_w[l], conv_b[l],
                  w_down[l])
```

```python
import functools

import jax
import jax.numpy as jnp
import numpy as np
from jax import lax
from jax.experimental import pallas as pl
from jax.experimental.pallas import tpu as pltpu

F32 = jnp.float32
BF16 = jnp.bfloat16

D_MODEL = 1024
GRID_W = 64
HG_HEADS = 4
HG_DK = 128
HG_W = HG_HEADS * HG_DK
HG_CHUNK = 64
AT_HEADS = 8
AT_KV_HEADS = 2
AT_GROUP = AT_HEADS // AT_KV_HEADS
AT_HD = 64
AT_W = AT_HEADS * AT_HD
AT_KV_W = AT_KV_HEADS * AT_HD
ROPE_THETA = 10000.0
ROPE_AXIS_DIM = AT_HD // 2
D_FF = 2816
EPS = 1e-6
LOG2_E = 1.4426950408889634

LANES = 128
BF16_SUBLANES = 16
VMEM_LIMIT = 56 * 1024 * 1024

VT_ROWS = AT_HD + BF16_SUBLANES
HALO = BF16_SUBLANES

TM_PROJ = 512
HG_SUB = 4 * HG_CHUNK
HG_ROWS = 512
TQ = 512
TK = 512
LOOKAHEAD = 2
KV_BLOCKS_PER_ITER = 4
TM_MERGE = 512
TM_FFN = 512
FF_CHUNK = 256


def _dot(a, b):
    return jnp.dot(a, b, preferred_element_type=F32)


def _dot_nt(a, b):
    return lax.dot_general(a, b, (((1,), (1,)), ((), ())), preferred_element_type=F32)


def _dot_tn(a, b):
    return lax.dot_general(a, b, (((0,), (0,)), ((), ())), preferred_element_type=F32)


def _sigmoid(x):
    return 1.0 / (1.0 + jnp.exp(-x))


def _rms_rows(x, g):
    ms = jnp.mean(x * x, axis=-1, keepdims=True)
    return x * lax.rsqrt(ms + EPS) * g


def _const_spec(shape):
    nd = len(shape)
    return pl.BlockSpec(shape, lambda *_: (0,) * nd)


def _lower_bound(lb_ref):
    a0 = lb_ref[0:1, :]
    a1 = lb_ref[1:2, :]
    m = jnp.maximum(a0, a1)
    e0 = jnp.exp(a0 - m)
    e1 = jnp.exp(a1 - m)
    return e0 / (e0 + e1)


def _head_rms(t, gsum):
    sq = t * t
    hi = sq.astype(BF16)
    lo = (sq - hi.astype(F32)).astype(BF16)
    ms = _dot(hi, gsum) + _dot(lo, gsum)
    return t * lax.rsqrt(ms + EPS)


def _rope(t, cos, sin_signed, first_half):
    from_hi = pltpu.roll(t, LANES - ROPE_AXIS_DIM // 2, axis=1)
    from_lo = pltpu.roll(t, ROPE_AXIS_DIM // 2, axis=1)
    return t * cos + jnp.where(first_half, from_hi, from_lo) * sin_signed


def _inproj_kernel(x_ref, n1_ref, whg_ref, wat_ref, lbf_ref, lbb_ref, qg_ref, kg_ref,
                   cos_ref, sin_ref, gsum_ref,
                   hq_ref, hv_ref, gf_ref, gb_ref, aq_ref, ak_ref, vt_ref):
    x = x_ref[0]
    u = _rms_rows(x, n1_ref[...]).astype(BF16)

    hq = _dot(u, whg_ref[:, 0:HG_W])
    hq_ref[0] = hq * _sigmoid(hq)
    hv_ref[0] = _dot(u, whg_ref[:, HG_W:2 * HG_W]).astype(BF16)
    for out_ref, lb_ref, off in ((gf_ref, lbf_ref, 2 * HG_W), (gb_ref, lbb_ref, 3 * HG_W)):
        pre = _dot(u, whg_ref[:, off:off + HG_W])
        lb = _lower_bound(lb_ref)
        out_ref[0] = jnp.log(lb + (1.0 - lb) * _sigmoid(pre))

    cos = cos_ref[...]
    sin = sin_ref[...]
    lane = lax.broadcasted_iota(jnp.int32, cos.shape, 1)
    first_half = (lane & (ROPE_AXIS_DIM - 1)) < (ROPE_AXIS_DIM // 2)
    sin_signed = jnp.where(first_half, -sin, sin)
    gsum = gsum_ref[...]

    aq = _dot(u, wat_ref[:, 0:AT_W])
    qn = _head_rms(aq, gsum)
    scale = AT_HD ** -0.5 * LOG2_E
    for c in range(AT_W // LANES):
        sl = slice(c * LANES, (c + 1) * LANES)
        t = qn[:, sl] * qg_ref[...]
        aq_ref[0, :, sl] = (_rope(t, cos, sin_signed, first_half) * scale).astype(BF16)

    ak = _dot(u, wat_ref[:, AT_W:AT_W + AT_KV_W])
    kn = _head_rms(ak, gsum[0:LANES, 0:LANES]) * kg_ref[...]
    kr = _rope(kn, cos, sin_signed, first_half).astype(BF16)
    for g in range(AT_KV_HEADS):
        ak_ref[0, g] = kr[:, g * AT_HD:(g + 1) * AT_HD]

    av = _dot(u, wat_ref[:, AT_W + AT_KV_W:AT_W + 2 * AT_KV_W])
    avt = av.T
    tm = x.shape[0]
    row = lax.broadcasted_iota(jnp.int32, (VT_ROWS - AT_HD, tm), 0)
    aug = jnp.where(row == 0, 1.0, 0.0).astype(BF16)
    for g in range(AT_KV_HEADS):
        vt_ref[0, g, 0:AT_HD, :] = avt[g * AT_HD:(g + 1) * AT_HD].astype(BF16)
        vt_ref[0, g, AT_HD:VT_ROWS, :] = aug


def _inproj(x, n1, whg, wat, lbf, lbb, qg2, kg2, cos, sin, gsum):
    B, L, D = x.shape
    tm = min(TM_PROJ, L)
    nt = L // tm
    tok = lambda w: pl.BlockSpec((1, tm, w), lambda b, i: (b, i, 0))
    out_shape = (
        jax.ShapeDtypeStruct((B, L, HG_W), F32),
        jax.ShapeDtypeStruct((B, L, HG_W), BF16),
        jax.ShapeDtypeStruct((B, L, HG_W), F32),
        jax.ShapeDtypeStruct((B, L, HG_W), F32),
        jax.ShapeDtypeStruct((B, L, AT_W), BF16),
        jax.ShapeDtypeStruct((B, AT_KV_HEADS, L, AT_HD), BF16),
        jax.ShapeDtypeStruct((B, AT_KV_HEADS, VT_ROWS, L), BF16),
    )
    out_specs = (
        tok(HG_W), tok(HG_W), tok(HG_W), tok(HG_W), tok(AT_W),
        pl.BlockSpec((1, AT_KV_HEADS, tm, AT_HD), lambda b, i: (b, 0, i, 0)),
        pl.BlockSpec((1, AT_KV_HEADS, VT_ROWS, tm), lambda b, i: (b, 0, 0, i)),
    )
    in_specs = [
        tok(D), _const_spec(n1.shape), _const_spec(whg.shape), _const_spec(wat.shape),
        _const_spec(lbf.shape), _const_spec(lbb.shape), _const_spec(qg2.shape), _const_spec(kg2.shape),
        pl.BlockSpec((tm, LANES), lambda b, i: (i, 0)),
        pl.BlockSpec((tm, LANES), lambda b, i: (i, 0)),
        _const_spec(gsum.shape),
    ]
    return pl.pallas_call(
        _inproj_kernel,
        grid=(B, nt),
        in_specs=in_specs,
        out_specs=out_specs,
        out_shape=out_shape,
        compiler_params=pltpu.CompilerParams(
            dimension_semantics=("arbitrary", "arbitrary"), vmem_limit_bytes=VMEM_LIMIT),
        name="inproj",
    )(x, n1, whg, wat, lbf, lbb, qg2, kg2, cos, sin, gsum)


def _hgrn_sub(q, v, g, st, reverse):
    n = q.shape[0]
    nchunk = n // HG_CHUNK
    shift = HG_CHUNK.bit_length() - 1
    r = lax.broadcasted_iota(jnp.int32, (n, n), 0)
    c = lax.broadcasted_iota(jnp.int32, (n, n), 1)
    same = lax.shift_right_logical(r, shift) == lax.shift_right_logical(c, shift)
    tri = same & ((c >= r) if reverse else (c <= r))
    trib = jnp.where(tri, 1.0, 0.0).astype(BF16)

    g1 = g.astype(BF16)
    rem = g - g1.astype(F32)
    g2 = rem.astype(BF16)
    g3 = (rem - g2.astype(F32)).astype(BF16)
    b = _dot(trib, g1) + _dot(trib, g2) + _dot(trib, g3)
    b3 = b.reshape(nchunk, HG_CHUNK, HG_DK)
    tot = b3[:, 0:1, :] if reverse else b3[:, HG_CHUNK - 1:HG_CHUNK, :]
    totf = jnp.broadcast_to(tot, b3.shape).reshape(n, HG_DK)

    k = 1.0 - jnp.exp(g)
    qd = (q * jnp.exp(b)).astype(BF16)
    kd = (k * jnp.exp(-b)).astype(BF16)
    kc = (k * jnp.exp(totf - b)).astype(BF16)
    dec = jnp.exp(tot)

    a = jnp.where(tri, _dot_nt(qd, kd), 0.0).astype(BF16)
    intra = _dot(a, v)

    outs = [None] * nchunk
    order = range(nchunk - 1, -1, -1) if reverse else range(nchunk)
    for ci in order:
        sl = slice(ci * HG_CHUNK, (ci + 1) * HG_CHUNK)
        outs[ci] = intra[sl] + _dot_nt(qd[sl], st.astype(BF16))
        st = st * dec[ci] + _dot_tn(v[sl], kc[sl])
    return jnp.concatenate(outs, axis=0), st


def _hgrn_dir(q_ref, v_ref, g_ref, o_ref, st_ref, reverse):
    rows = q_ref.shape[1]
    nsub = rows // HG_SUB
    st = st_ref[...]
    order = range(nsub - 1, -1, -1) if reverse else range(nsub)
    for s in order:
        sl = slice(s * HG_SUB, (s + 1) * HG_SUB)
        o, st = _hgrn_sub(q_ref[0, sl, :], v_ref[0, sl, :], g_ref[0, sl, :], st, reverse)
        o_ref[0, sl, :] = o
    st_ref[...] = st


def _hgrn_kernel(qf_ref, vf_ref, gf_ref, qb_ref, vb_ref, gb_ref, of_ref, ob_ref, stf_ref, stb_ref):
    @pl.when(pl.program_id(2) == 0)
    def _():
        stf_ref[...] = jnp.zeros_like(stf_ref)
        stb_ref[...] = jnp.zeros_like(stb_ref)

    _hgrn_dir(qf_ref, vf_ref, gf_ref, of_ref, stf_ref, False)
    _hgrn_dir(qb_ref, vb_ref, gb_ref, ob_ref, stb_ref, True)


def _hgrn(hq, hv, gf, gb):
    B, L, _ = hq.shape
    rows = min(HG_ROWS, L)
    nb = L // rows
    fwd = pl.BlockSpec((1, rows, HG_DK), lambda b, h, i: (b, i, h))
    bwd = pl.BlockSpec((1, rows, HG_DK), lambda b, h, i: (b, nb - 1 - i, h))
    o = jax.ShapeDtypeStruct((B, L, HG_W), F32)
    return pl.pallas_call(
        _hgrn_kernel,
        grid=(B, HG_HEADS, nb),
        in_specs=[fwd, fwd, fwd, bwd, bwd, bwd],
        out_specs=(fwd, bwd),
        out_shape=(o, o),
        scratch_shapes=[pltpu.VMEM((HG_DK, HG_DK), F32), pltpu.VMEM((HG_DK, HG_DK), F32)],
        compiler_params=pltpu.CompilerParams(
            dimension_semantics=("arbitrary", "arbitrary", "arbitrary"), vmem_limit_bytes=VMEM_LIMIT),
        name="hgrn2",
    )(hq, hv, gf, hq, hv, gb)


def _attn_kernel(q_ref, k_ref, vt_ref, o_ref, *, tk, inner):
    tq = q_ref.shape[1]
    nk = k_ref.shape[2] // tk
    qs = [q_ref[0, :, j * AT_HD:(j + 1) * AT_HD] for j in range(AT_GROUP)]

    def keys(blk):
        start = pl.multiple_of(blk * tk, tk)
        return k_ref[0, 0, pl.ds(start, tk), :]

    def scores(blk, j):
        return _dot_nt(keys(blk), qs[j])

    def body(it, carry):
        pending, state = carry
        pending = list(pending)
        state = list(state)
        for n in range(inner * AT_GROUP):
            blk = it * inner + n // AT_GROUP
            j = n % AT_GROUP
            start = pl.multiple_of(blk * tk, tk)
            vb = vt_ref[0, 0, :, pl.ds(start, tk)]
            m, acc = state[j]
            st = pending.pop(0)
            ahead = n + LOOKAHEAD
            pending.append(scores(jnp.minimum(it * inner + ahead // AT_GROUP, nk - 1), ahead % AT_GROUP))
            m_new = jnp.maximum(m, jnp.max(st, axis=0, keepdims=True))
            alpha = jnp.exp2(m - m_new)
            p = jnp.exp2(st - m_new).astype(BF16)
            state[j] = (m_new, alpha * acc + _dot(vb, p))
        return tuple(pending), tuple(state)

    init = tuple((jnp.full((1, tq), -1e30, F32), jnp.zeros((VT_ROWS, tq), F32)) for _ in range(AT_GROUP))
    first = tuple(scores(min(n // AT_GROUP, nk - 1), n % AT_GROUP) for n in range(LOOKAHEAD))
    _, final = lax.fori_loop(0, nk // inner, body, (first, init))
    normed = [acc[0:AT_HD] / acc[AT_HD:AT_HD + 1] for _, acc in final]
    pairs = [jnp.concatenate(normed[2 * p:2 * p + 2], axis=0).T for p in range(AT_GROUP // 2)]
    o_ref[0] = jnp.concatenate(pairs, axis=1).astype(o_ref.dtype)


def _attention(aq, ak, vt):
    B, L, _ = aq.shape
    tq = min(TQ, L)
    tk = min(TK, L)
    gw = AT_GROUP * AT_HD
    return pl.pallas_call(
        functools.partial(_attn_kernel, tk=tk, inner=min(KV_BLOCKS_PER_ITER, L // tk)),
        grid=(B, AT_KV_HEADS, L // tq),
        in_specs=[
            pl.BlockSpec((1, tq, gw), lambda b, g, i: (b, i, g)),
            pl.BlockSpec((1, 1, L, AT_HD), lambda b, g, i: (b, g, 0, 0)),
            pl.BlockSpec((1, 1, VT_ROWS, L), lambda b, g, i: (b, g, 0, 0)),
        ],
        out_specs=pl.BlockSpec((1, tq, gw), lambda b, g, i: (b, i, g)),
        out_shape=jax.ShapeDtypeStruct((B, L, AT_W), BF16),
        compiler_params=pltpu.CompilerParams(
            dimension_semantics=("arbitrary", "arbitrary", "arbitrary"), vmem_limit_bytes=VMEM_LIMIT),
        name="attention",
    )(aq, ak, vt)


def _merge_kernel(x_ref, of_ref, ob_ref, oat_ref, n1_ref, wg_ref, ong_ref, wa_ref, wb_ref, wo_ref,
                  n2_ref, h_ref, hn_ref):
    x = x_ref[0]
    u = _rms_rows(x, n1_ref[...]).astype(BF16)
    hgate = _dot(u, wg_ref[:, 0:HG_W])
    o = of_ref[0] + ob_ref[0]
    heads = []
    for h in range(HG_HEADS):
        sl = slice(h * HG_DK, (h + 1) * HG_DK)
        heads.append(_rms_rows(o[:, sl], ong_ref[...]))
    oa = (jnp.concatenate(heads, axis=1) * (hgate * _sigmoid(hgate))).astype(BF16)
    ya = _dot(oa, wa_ref[...])
    yb = _dot(oat_ref[0], wb_ref[...])
    ga = _dot(u, wg_ref[:, HG_W:HG_W + D_MODEL])
    gb = _dot(u, wg_ref[:, HG_W + D_MODEL:HG_W + 2 * D_MODEL])
    merged = (_sigmoid(ga) * ya + _sigmoid(gb) * yb).astype(BF16)
    h = x + _dot(merged, wo_ref[...])
    h_ref[0] = h
    hn_ref[0] = _rms_rows(h, n2_ref[...]).astype(BF16)


def _merge(x, o_f, o_b, o_at, n1, wg, ong, wa, wb, wo, n2):
    B, L, D = x.shape
    tm = min(TM_MERGE, L)
    tok = lambda w: pl.BlockSpec((1, tm, w), lambda b, i: (b, i, 0))
    return pl.pallas_call(
        _merge_kernel,
        grid=(B, L // tm),
        in_specs=[tok(D), tok(HG_W), tok(HG_W), tok(AT_W), _const_spec(n1.shape), _const_spec(wg.shape),
                  _const_spec(ong.shape), _const_spec(wa.shape), _const_spec(wb.shape),
                  _const_spec(wo.shape), _const_spec(n2.shape)],
        out_specs=(tok(D), tok(D)),
        out_shape=(jax.ShapeDtypeStruct((B, L, D), F32), jax.ShapeDtypeStruct((B, L, D), BF16)),
        compiler_params=pltpu.CompilerParams(
            dimension_semantics=("arbitrary", "arbitrary"), vmem_limit_bytes=VMEM_LIMIT),
        name="merge",
    )(x, o_f, o_b, o_at, n1, wg, ong, wa, wb, wo, n2)


def _ffn_kernel(h_ref, hn_ref, prev_ref, next_ref, wup_ref, cw_ref, cb_ref, wdn_ref, y_ref, ext_ref):
    i = pl.program_id(1)
    nt = pl.num_programs(1)
    tm = hn_ref.shape[1]
    ext = tm + 2 * HALO
    ext_ref[0:HALO, :] = jnp.where(i > 0, prev_ref[0], jnp.zeros_like(prev_ref[0]))
    ext_ref[HALO:HALO + tm, :] = hn_ref[0]
    ext_ref[HALO + tm:ext, :] = jnp.where(i < nt - 1, next_ref[0], jnp.zeros_like(next_ref[0]))
    hx = ext_ref[...]

    def conv(t, off):
        w = cw_ref[:, off:off + FF_CHUNK]
        before = pltpu.roll(t, 1, axis=0)[HALO:HALO + tm]
        after = pltpu.roll(t, ext - 1, axis=0)[HALO:HALO + tm]
        return (before * w[0:1] + t[HALO:HALO + tm] * w[1:2] + after * w[2:3]
                + cb_ref[:, off:off + FF_CHUNK])

    acc = jnp.zeros((tm, D_MODEL), F32)
    for j in range(D_FF // FF_CHUNK):
        off = j * FF_CHUNK
        val = conv(_dot(hx, wup_ref[:, off:off + FF_CHUNK]), off)
        gate = conv(_dot(hx, wup_ref[:, D_FF + off:D_FF + off + FF_CHUNK]), D_FF + off)
        act = (gate * _sigmoid(gate) * val).astype(BF16)
        acc = acc + _dot(act, wdn_ref[off:off + FF_CHUNK, :])
    y_ref[0] = h_ref[0] + acc


def _ffn(h, hn, wup, cw, cb, wdn):
    B, L, D = h.shape
    tm = min(TM_FFN, L)
    per = tm // HALO
    nh = L // HALO
    tok = pl.BlockSpec((1, tm, D), lambda b, i: (b, i, 0))
    return pl.pallas_call(
        _ffn_kernel,
        grid=(B, L // tm),
        in_specs=[
            tok, tok,
            pl.BlockSpec((1, HALO, D), lambda b, i: (b, jnp.maximum(i * per - 1, 0), 0)),
            pl.BlockSpec((1, HALO, D), lambda b, i: (b, jnp.minimum((i + 1) * per, nh - 1), 0)),
            _const_spec(wup.shape), _const_spec(cw.shape), _const_spec(cb.shape), _const_spec(wdn.shape),
        ],
        out_specs=tok,
        out_shape=jax.ShapeDtypeStruct((B, L, D), F32),
        scratch_shapes=[pltpu.VMEM((tm + 2 * HALO, D), BF16)],
        compiler_params=pltpu.CompilerParams(
            dimension_semantics=("arbitrary", "arbitrary"), vmem_limit_bytes=VMEM_LIMIT),
        name="convffn",
    )(h, hn, hn, hn, wup, cw, cb, wdn)


def _rope_tables(L):
    pos = jnp.arange(L, dtype=jnp.int32)
    pos_row = (pos // GRID_W).astype(F32)
    pos_col = (pos % GRID_W).astype(F32)
    inv = ROPE_THETA ** (-jnp.arange(0, ROPE_AXIS_DIM, 2, dtype=F32) / ROPE_AXIS_DIM)
    ang_r = pos_row[:, None] * inv[None, :]
    ang_c = pos_col[:, None] * inv[None, :]
    ang = jnp.concatenate([ang_r, ang_r, ang_c, ang_c], axis=-1)
    ang = jnp.concatenate([ang] * (LANES // AT_HD), axis=-1)
    return jnp.cos(ang), jnp.sin(ang)


def _layer(x, norm1_g, w_in, lb_f, lb_b, onorm_g, q_norm_g, k_norm_g, w_a, w_b, w_out, norm2_g,
           w_up, conv_w, conv_b, w_down):
    B, L, D = x.shape
    c_hg = 4 * HG_W
    c_gate = 5 * HG_W
    c_at = c_gate + AT_W + 2 * AT_KV_W
    whg = w_in[:, 0:c_hg].astype(BF16)
    wat = w_in[:, c_gate:c_at].astype(BF16)
    wg = jnp.concatenate([w_in[:, c_hg:c_gate], w_in[:, c_at:]], axis=1).astype(BF16)
    n1 = norm1_g.reshape(1, D)
    n2 = norm2_g.reshape(1, D)
    qg2 = jnp.tile(q_norm_g.reshape(1, AT_HD), (1, LANES // AT_HD))
    kg2 = jnp.tile(k_norm_g.reshape(1, AT_HD), (1, LANES // AT_HD))
    cos, sin = _rope_tables(L)
    blk = np.arange(AT_W) // AT_HD
    gsum = jnp.asarray((blk[:, None] == blk[None, :]).astype(np.float32) / AT_HD, dtype=BF16)

    hq, hv, gf, gb, aq, ak, vt = _inproj(x, n1, whg, wat, lb_f, lb_b, qg2, kg2, cos, sin, gsum)
    o_f, o_b = _hgrn(hq, hv, gf, gb)
    o_at = _attention(aq, ak, vt)
    h, hn = _merge(x, o_f, o_b, o_at, n1, wg, onorm_g.reshape(1, HG_DK), w_a.astype(BF16),
                   w_b.astype(BF16), w_out.astype(BF16), n2)
    return _ffn(h, hn, w_up.astype(BF16), conv_w, conv_b.reshape(1, 2 * D_FF), w_down.astype(BF16))


def kernel(x, norm1_g, w_in, hg_lb_fwd, hg_lb_bwd, hg_onorm_g, q_norm_g, k_norm_g, w_branch_a,
           w_branch_b, w_out, norm2_g, w_up, conv_w, conv_b, w_down):
    depth = w_in.shape[0]
    assert depth == 1 and hg_lb_fwd.shape[0] == 2
    l = 0
    return _layer(x, norm1_g[l], w_in[l], hg_lb_fwd, hg_lb_bwd, hg_onorm_g[l], q_norm_g[l], k_norm_g[l],
                  w_branch_a[l], w_branch_b[l], w_out[l], norm2_g[l], w_up[l], conv_w[l], conv_b[l],
                  w_down[l])
```

```python
import functools

import jax
import jax.numpy as jnp
import numpy as np
from jax import lax
from jax.experimental import pallas as pl
from jax.experimental.pallas import tpu as pltpu

F32 = jnp.float32
BF16 = jnp.bfloat16

D_MODEL = 1024
GRID_W = 64
HG_HEADS = 4
HG_DK = 128
HG_W = HG_HEADS * HG_DK
HG_CHUNK = 64
AT_HEADS = 8
AT_KV_HEADS = 2
AT_GROUP = AT_HEADS // AT_KV_HEADS
AT_HD = 64
AT_W = AT_HEADS * AT_HD
AT_KV_W = AT_KV_HEADS * AT_HD
ROPE_THETA = 10000.0
ROPE_AXIS_DIM = AT_HD // 2
D_FF = 2816
EPS = 1e-6
LOG2_E = 1.4426950408889634

LANES = 128
BF16_SUBLANES = 16
VMEM_LIMIT = 56 * 1024 * 1024

VT_ROWS = AT_HD + BF16_SUBLANES
HALO = BF16_SUBLANES

TM_PROJ = 512
HG_SUB = 4 * HG_CHUNK
HG_ROWS = 512
TQ = 512
TK = 512
LOOKAHEAD = 2
LOOKAHEAD_BOUNDED = 1
SCORE_LIMIT = 60.0
KV_BLOCKS_PER_ITER = 4
TM_MERGE = 512
TM_FFN = 512
FF_CHUNK = 256


def _dot(a, b):
    return jnp.dot(a, b, preferred_element_type=F32)


def _dot_nt(a, b):
    return lax.dot_general(a, b, (((1,), (1,)), ((), ())), preferred_element_type=F32)


def _dot_tn(a, b):
    return lax.dot_general(a, b, (((0,), (0,)), ((), ())), preferred_element_type=F32)


def _sigmoid(x):
    return 1.0 / (1.0 + jnp.exp(-x))


def _rms_rows(x, g):
    ms = jnp.mean(x * x, axis=-1, keepdims=True)
    return x * lax.rsqrt(ms + EPS) * g


def _const_spec(shape):
    nd = len(shape)
    return pl.BlockSpec(shape, lambda *_: (0,) * nd)


def _lower_bound(lb_ref):
    a0 = lb_ref[0:1, :]
    a1 = lb_ref[1:2, :]
    m = jnp.maximum(a0, a1)
    e0 = jnp.exp(a0 - m)
    e1 = jnp.exp(a1 - m)
    return e0 / (e0 + e1)


def _head_rms(t, gsum):
    sq = t * t
    hi = sq.astype(BF16)
    lo = (sq - hi.astype(F32)).astype(BF16)
    ms = _dot(hi, gsum) + _dot(lo, gsum)
    return t * lax.rsqrt(ms + EPS)


def _rope(t, cos, sin_signed, first_half):
    from_hi = pltpu.roll(t, LANES - ROPE_AXIS_DIM // 2, axis=1)
    from_lo = pltpu.roll(t, ROPE_AXIS_DIM // 2, axis=1)
    return t * cos + jnp.where(first_half, from_hi, from_lo) * sin_signed


def _inproj_kernel(x_ref, n1_ref, whg_ref, wat_ref, lbf_ref, lbb_ref, qg_ref, kg_ref,
                   cos_ref, sin_ref, gsum_ref,
                   hq_ref, hv_ref, gf_ref, gb_ref, aq_ref, ak_ref, vt_ref):
    x = x_ref[0]
    u = _rms_rows(x, n1_ref[...]).astype(BF16)

    hq = _dot(u, whg_ref[:, 0:HG_W])
    hq_ref[0] = hq * _sigmoid(hq)
    hv_ref[0] = _dot(u, whg_ref[:, HG_W:2 * HG_W]).astype(BF16)
    for out_ref, lb_ref, off in ((gf_ref, lbf_ref, 2 * HG_W), (gb_ref, lbb_ref, 3 * HG_W)):
        pre = _dot(u, whg_ref[:, off:off + HG_W])
        lb = _lower_bound(lb_ref)
        out_ref[0] = jnp.log(lb + (1.0 - lb) * _sigmoid(pre))

    cos = cos_ref[...]
    sin = sin_ref[...]
    lane = lax.broadcasted_iota(jnp.int32, cos.shape, 1)
    first_half = (lane & (ROPE_AXIS_DIM - 1)) < (ROPE_AXIS_DIM // 2)
    sin_signed = jnp.where(first_half, -sin, sin)
    gsum = gsum_ref[...]

    aq = _dot(u, wat_ref[:, 0:AT_W])
    qn = _head_rms(aq, gsum)
    scale = AT_HD ** -0.5 * LOG2_E
    for c in range(AT_W // LANES):
        sl = slice(c * LANES, (c + 1) * LANES)
        t = qn[:, sl] * qg_ref[...]
        aq_ref[0, :, sl] = (_rope(t, cos, sin_signed, first_half) * scale).astype(BF16)

    ak = _dot(u, wat_ref[:, AT_W:AT_W + AT_KV_W])
    kn = _head_rms(ak, gsum[0:LANES, 0:LANES]) * kg_ref[...]
    kr = _rope(kn, cos, sin_signed, first_half).astype(BF16)
    for g in range(AT_KV_HEADS):
        ak_ref[0, g] = kr[:, g * AT_HD:(g + 1) * AT_HD]

    av = _dot(u, wat_ref[:, AT_W + AT_KV_W:AT_W + 2 * AT_KV_W])
    avt = av.T
    tm = x.shape[0]
    row = lax.broadcasted_iota(jnp.int32, (VT_ROWS - AT_HD, tm), 0)
    aug = jnp.where(row == 0, 1.0, 0.0).astype(BF16)
    for g in range(AT_KV_HEADS):
        vt_ref[0, g, 0:AT_HD, :] = avt[g * AT_HD:(g + 1) * AT_HD].astype(BF16)
        vt_ref[0, g, AT_HD:VT_ROWS, :] = aug


def _inproj(x, n1, whg, wat, lbf, lbb, qg2, kg2, cos, sin, gsum):
    B, L, D = x.shape
    tm = min(TM_PROJ, L)
    nt = L // tm
    tok = lambda w: pl.BlockSpec((1, tm, w), lambda b, i: (b, i, 0))
    out_shape = (
        jax.ShapeDtypeStruct((B, L, HG_W), F32),
        jax.ShapeDtypeStruct((B, L, HG_W), BF16),
        jax.ShapeDtypeStruct((B, L, HG_W), F32),
        jax.ShapeDtypeStruct((B, L, HG_W), F32),
        jax.ShapeDtypeStruct((B, L, AT_W), BF16),
        jax.ShapeDtypeStruct((B, AT_KV_HEADS, L, AT_HD), BF16),
        jax.ShapeDtypeStruct((B, AT_KV_HEADS, VT_ROWS, L), BF16),
    )
    out_specs = (
        tok(HG_W), tok(HG_W), tok(HG_W), tok(HG_W), tok(AT_W),
        pl.BlockSpec((1, AT_KV_HEADS, tm, AT_HD), lambda b, i: (b, 0, i, 0)),
        pl.BlockSpec((1, AT_KV_HEADS, VT_ROWS, tm), lambda b, i: (b, 0, 0, i)),
    )
    in_specs = [
        tok(D), _const_spec(n1.shape), _const_spec(whg.shape), _const_spec(wat.shape),
        _const_spec(lbf.shape), _const_spec(lbb.shape), _const_spec(qg2.shape), _const_spec(kg2.shape),
        pl.BlockSpec((tm, LANES), lambda b, i: (i, 0)),
        pl.BlockSpec((tm, LANES), lambda b, i: (i, 0)),
        _const_spec(gsum.shape),
    ]
    return pl.pallas_call(
        _inproj_kernel,
        grid=(B, nt),
        in_specs=in_specs,
        out_specs=out_specs,
        out_shape=out_shape,
        compiler_params=pltpu.CompilerParams(
            dimension_semantics=("arbitrary", "arbitrary"), vmem_limit_bytes=VMEM_LIMIT),
        name="inproj",
    )(x, n1, whg, wat, lbf, lbb, qg2, kg2, cos, sin, gsum)


def _chunk_mask(n, reverse):
    shift = HG_CHUNK.bit_length() - 1
    r = lax.broadcasted_iota(jnp.int32, (n, n), 0)
    c = lax.broadcasted_iota(jnp.int32, (n, n), 1)
    same = lax.shift_right_logical(r, shift) == lax.shift_right_logical(c, shift)
    return same & ((c >= r) if reverse else (c <= r))


def _split3(x):
    x1 = x.astype(BF16)
    rem = x - x1.astype(F32)
    x2 = rem.astype(BF16)
    return x1, x2, (rem - x2.astype(F32)).astype(BF16)


def _hgrn_kernel(qf_ref, vf_ref, gf_ref, qb_ref, vb_ref, gb_ref, of_ref, ob_ref, stf_ref, stb_ref):
    @pl.when(pl.program_id(1) == 0)
    def _():
        stf_ref[...] = jnp.zeros_like(stf_ref)
        stb_ref[...] = jnp.zeros_like(stb_ref)

    n = HG_SUB
    nsub = qf_ref.shape[1] // n
    nchunk = n // HG_CHUNK
    tri = {rev: _chunk_mask(n, rev) for rev in (False, True)}
    trib = {rev: jnp.where(m, 1.0, 0.0).astype(BF16) for rev, m in tri.items()}

    groups = []
    for h in range(HG_HEADS):
        cols = slice(h * HG_DK, (h + 1) * HG_DK)
        for s in range(nsub):
            groups.append(((qf_ref, vf_ref, gf_ref, of_ref), False, h, slice(s * n, (s + 1) * n), cols))
        for s in range(nsub - 1, -1, -1):
            groups.append(((qb_ref, vb_ref, gb_ref, ob_ref), True, h, slice(s * n, (s + 1) * n), cols))

    gs = [refs[2][0, rs, cs] for refs, _, _, rs, cs in groups]
    bs = []
    for g, (_, rev, _, _, _) in zip(gs, groups):
        g1, g2, g3 = _split3(g)
        bs.append(_dot(trib[rev], g1) + _dot(trib[rev], g2) + _dot(trib[rev], g3))

    qds, kds, kcs, decs = [], [], [], []
    for g, b, (refs, rev, _, rs, cs) in zip(gs, bs, groups):
        b3 = b.reshape(nchunk, HG_CHUNK, HG_DK)
        tot = b3[:, 0:1, :] if rev else b3[:, HG_CHUNK - 1:HG_CHUNK, :]
        totf = jnp.broadcast_to(tot, b3.shape).reshape(n, HG_DK)
        k = 1.0 - jnp.exp(g)
        qds.append((refs[0][0, rs, cs] * jnp.exp(b)).astype(BF16))
        kds.append((k * jnp.exp(-b)).astype(BF16))
        kcs.append((k * jnp.exp(totf - b)).astype(BF16))
        decs.append(jnp.exp(tot))

    amats = [_dot_nt(qd, kd) for qd, kd in zip(qds, kds)]
    vs = [refs[1][0, rs, cs] for refs, _, _, rs, cs in groups]
    intras = [_dot(jnp.where(tri[rev], a, 0.0).astype(BF16), v)
              for a, v, (_, rev, _, _, _) in zip(amats, vs, groups)]
    chunks = [slice(c * HG_CHUNK, (c + 1) * HG_CHUNK) for c in range(nchunk)]
    uts = [[_dot_tn(v[sl], kc[sl]) for sl in chunks] for v, kc in zip(vs, kcs)]

    states = {}
    for gi, (refs, rev, h, rs, cs) in enumerate(groups):
        st = states[(rev, h)] if (rev, h) in states else (stb_ref if rev else stf_ref)[h]
        outs = [None] * nchunk
        for ci in (range(nchunk - 1, -1, -1) if rev else range(nchunk)):
            outs[ci] = intras[gi][chunks[ci]] + _dot_nt(qds[gi][chunks[ci]], st.astype(BF16))
            st = st * decs[gi][ci] + uts[gi][ci]
        states[(rev, h)] = st
        refs[3][0, rs, cs] = jnp.concatenate(outs, axis=0)
    for (rev, h), st in states.items():
        (stb_ref if rev else stf_ref)[h] = st


def _hgrn(hq, hv, gf, gb):
    B, L, _ = hq.shape
    rows = min(HG_ROWS, L)
    nb = L // rows
    fwd = pl.BlockSpec((1, rows, HG_W), lambda b, i: (b, i, 0))
    bwd = pl.BlockSpec((1, rows, HG_W), lambda b, i: (b, nb - 1 - i, 0))
    o = jax.ShapeDtypeStruct((B, L, HG_W), F32)
    state = pltpu.VMEM((HG_HEADS, HG_DK, HG_DK), F32)
    return pl.pallas_call(
        _hgrn_kernel,
        grid=(B, nb),
        in_specs=[fwd, fwd, fwd, bwd, bwd, bwd],
        out_specs=(fwd, bwd),
        out_shape=(o, o),
        scratch_shapes=[state, state],
        compiler_params=pltpu.CompilerParams(
            dimension_semantics=("arbitrary", "arbitrary"), vmem_limit_bytes=VMEM_LIMIT),
        name="hgrn2",
    )(hq, hv, gf, hq, hv, gb)


def _attn_kernel(flag_ref, q_ref, k_ref, vt_ref, o_ref, acc_ref, *, tk, inner):
    tq = q_ref.shape[1]
    nk = k_ref.shape[2] // tk
    qs = [q_ref[0, :, j * AT_HD:(j + 1) * AT_HD] for j in range(AT_GROUP)]

    def scores(blk, j, stabilised):
        start = pl.multiple_of(blk * tk, tk)
        st = _dot_nt(k_ref[0, 0, pl.ds(start, tk), :], qs[j])
        return st if stabilised else jnp.exp2(st).astype(BF16)

    def run(stabilised, lookahead):
        def body(it, carry):
            pending, state = carry
            pending = list(pending)
            state = list(state)
            for n in range(inner * AT_GROUP):
                blk = it * inner + n // AT_GROUP
                j = n % AT_GROUP
                start = pl.multiple_of(blk * tk, tk)
                vb = vt_ref[0, 0, :, pl.ds(start, tk)]
                st = pending.pop(0)
                ahead = n + lookahead
                pending.append(scores(jnp.minimum(it * inner + ahead // AT_GROUP, nk - 1), ahead % AT_GROUP,
                                      stabilised))
                if stabilised:
                    m, acc = state[j]
                    m_new = jnp.maximum(m, jnp.max(st, axis=0, keepdims=True))
                    p = jnp.exp2(st - m_new).astype(BF16)
                    state[j] = (m_new, jnp.exp2(m - m_new) * acc + _dot(vb, p))
                else:
                    state[j] = state[j] + _dot(vb, st)
            return tuple(pending), tuple(state)

        zero = jnp.zeros((VT_ROWS, tq), F32)
        init = tuple((jnp.full((1, tq), -1e30, F32), zero) if stabilised else zero for _ in range(AT_GROUP))
        first = tuple(scores(min(n // AT_GROUP, nk - 1), n % AT_GROUP, stabilised) for n in range(lookahead))
        _, final = lax.fori_loop(0, nk // inner, body, (first, init))
        for j in range(AT_GROUP):
            acc_ref[j] = final[j][1] if stabilised else final[j]

    @pl.when(flag_ref[0] != 0)
    def _():
        run(False, LOOKAHEAD_BOUNDED)

    @pl.when(flag_ref[0] == 0)
    def _():
        run(True, LOOKAHEAD)

    normed = [acc_ref[j, 0:AT_HD, :] / acc_ref[j, AT_HD:AT_HD + 1, :] for j in range(AT_GROUP)]
    pairs = [jnp.concatenate(normed[2 * p:2 * p + 2], axis=0).T for p in range(AT_GROUP // 2)]
    o_ref[0] = jnp.concatenate(pairs, axis=1).astype(o_ref.dtype)


def _attention(flag, aq, ak, vt):
    B, L, _ = aq.shape
    tq = min(TQ, L)
    tk = min(TK, L)
    gw = AT_GROUP * AT_HD
    return pl.pallas_call(
        functools.partial(_attn_kernel, tk=tk, inner=min(KV_BLOCKS_PER_ITER, L // tk)),
        grid=(B, AT_KV_HEADS, L // tq),
        in_specs=[
            pl.BlockSpec(memory_space=pltpu.SMEM),
            pl.BlockSpec((1, tq, gw), lambda b, g, i: (b, i, g)),
            pl.BlockSpec((1, 1, L, AT_HD), lambda b, g, i: (b, g, 0, 0)),
            pl.BlockSpec((1, 1, VT_ROWS, L), lambda b, g, i: (b, g, 0, 0)),
        ],
        out_specs=pl.BlockSpec((1, tq, gw), lambda b, g, i: (b, i, g)),
        out_shape=jax.ShapeDtypeStruct((B, L, AT_W), BF16),
        scratch_shapes=[pltpu.VMEM((AT_GROUP, VT_ROWS, tq), F32)],
        compiler_params=pltpu.CompilerParams(
            dimension_semantics=("arbitrary", "arbitrary", "arbitrary"), vmem_limit_bytes=VMEM_LIMIT),
        name="attention",
    )(flag, aq, ak, vt)


def _merge_kernel(x_ref, of_ref, ob_ref, oat_ref, n1_ref, wg_ref, ong_ref, wa_ref, wb_ref, wo_ref,
                  n2_ref, h_ref, hn_ref):
    x = x_ref[0]
    u = _rms_rows(x, n1_ref[...]).astype(BF16)
    hgate = _dot(u, wg_ref[:, 0:HG_W])
    o = of_ref[0] + ob_ref[0]
    heads = []
    for h in range(HG_HEADS):
        sl = slice(h * HG_DK, (h + 1) * HG_DK)
        heads.append(_rms_rows(o[:, sl], ong_ref[...]))
    oa = (jnp.concatenate(heads, axis=1) * (hgate * _sigmoid(hgate))).astype(BF16)
    ya = _dot(oa, wa_ref[...])
    yb = _dot(oat_ref[0], wb_ref[...])
    ga = _dot(u, wg_ref[:, HG_W:HG_W + D_MODEL])
    gb = _dot(u, wg_ref[:, HG_W + D_MODEL:HG_W + 2 * D_MODEL])
    merged = (_sigmoid(ga) * ya + _sigmoid(gb) * yb).astype(BF16)
    h = x + _dot(merged, wo_ref[...])
    h_ref[0] = h
    hn_ref[0] = _rms_rows(h, n2_ref[...]).astype(BF16)


def _merge(x, o_f, o_b, o_at, n1, wg, ong, wa, wb, wo, n2):
    B, L, D = x.shape
    tm = min(TM_MERGE, L)
    tok = lambda w: pl.BlockSpec((1, tm, w), lambda b, i: (b, i, 0))
    return pl.pallas_call(
        _merge_kernel,
        grid=(B, L // tm),
        in_specs=[tok(D), tok(HG_W), tok(HG_W), tok(AT_W), _const_spec(n1.shape), _const_spec(wg.shape),
                  _const_spec(ong.shape), _const_spec(wa.shape), _const_spec(wb.shape),
                  _const_spec(wo.shape), _const_spec(n2.shape)],
        out_specs=(tok(D), tok(D)),
        out_shape=(jax.ShapeDtypeStruct((B, L, D), F32), jax.ShapeDtypeStruct((B, L, D), BF16)),
        compiler_params=pltpu.CompilerParams(
            dimension_semantics=("arbitrary", "arbitrary"), vmem_limit_bytes=VMEM_LIMIT),
        name="merge",
    )(x, o_f, o_b, o_at, n1, wg, ong, wa, wb, wo, n2)


def _ffn_kernel(h_ref, hn_ref, prev_ref, next_ref, wup_ref, cw_ref, cb_ref, wdn_ref, y_ref, ext_ref):
    i = pl.program_id(1)
    nt = pl.num_programs(1)
    tm = hn_ref.shape[1]
    ext = tm + 2 * HALO
    ext_ref[0:HALO, :] = jnp.where(i > 0, prev_ref[0], jnp.zeros_like(prev_ref[0]))
    ext_ref[HALO:HALO + tm, :] = hn_ref[0]
    ext_ref[HALO + tm:ext, :] = jnp.where(i < nt - 1, next_ref[0], jnp.zeros_like(next_ref[0]))
    hx = ext_ref[...]

    def conv(t, off):
        w = cw_ref[:, off:off + FF_CHUNK]
        before = pltpu.roll(t, 1, axis=0)[HALO:HALO + tm]
        after = pltpu.roll(t, ext - 1, axis=0)[HALO:HALO + tm]
        return (before * w[0:1] + t[HALO:HALO + tm] * w[1:2] + after * w[2:3]
                + cb_ref[:, off:off + FF_CHUNK])

    def up(j):
        off = j * FF_CHUNK
        return (_dot(hx, wup_ref[:, off:off + FF_CHUNK]),
                _dot(hx, wup_ref[:, D_FF + off:D_FF + off + FF_CHUNK]))

    nchunk = D_FF // FF_CHUNK
    acc = jnp.zeros((tm, D_MODEL), F32)
    nxt = up(0)
    for j in range(nchunk):
        off = j * FF_CHUNK
        upv, upg = nxt
        if j + 1 < nchunk:
            nxt = up(j + 1)
        val = conv(upv, off)
        gate = conv(upg, D_FF + off)
        act = (gate * _sigmoid(gate) * val).astype(BF16)
        acc = acc + _dot(act, wdn_ref[off:off + FF_CHUNK, :])
    y_ref[0] = h_ref[0] + acc


def _ffn(h, hn, wup, cw, cb, wdn):
    B, L, D = h.shape
    tm = min(TM_FFN, L)
    per = tm // HALO
    nh = L // HALO
    tok = pl.BlockSpec((1, tm, D), lambda b, i: (b, i, 0))
    return pl.pallas_call(
        _ffn_kernel,
        grid=(B, L // tm),
        in_specs=[
            tok, tok,
            pl.BlockSpec((1, HALO, D), lambda b, i: (b, jnp.maximum(i * per - 1, 0), 0)),
            pl.BlockSpec((1, HALO, D), lambda b, i: (b, jnp.minimum((i + 1) * per, nh - 1), 0)),
            _const_spec(wup.shape), _const_spec(cw.shape), _const_spec(cb.shape), _const_spec(wdn.shape),
        ],
        out_specs=tok,
        out_shape=jax.ShapeDtypeStruct((B, L, D), F32),
        scratch_shapes=[pltpu.VMEM((tm + 2 * HALO, D), BF16)],
        compiler_params=pltpu.CompilerParams(
            dimension_semantics=("arbitrary", "arbitrary"), vmem_limit_bytes=VMEM_LIMIT),
        name="convffn",
    )(h, hn, hn, hn, wup, cw, cb, wdn)


def _rope_tables(L):
    pos = jnp.arange(L, dtype=jnp.int32)
    pos_row = (pos // GRID_W).astype(F32)
    pos_col = (pos % GRID_W).astype(F32)
    inv = ROPE_THETA ** (-jnp.arange(0, ROPE_AXIS_DIM, 2, dtype=F32) / ROPE_AXIS_DIM)
    ang_r = pos_row[:, None] * inv[None, :]
    ang_c = pos_col[:, None] * inv[None, :]
    ang = jnp.concatenate([ang_r, ang_r, ang_c, ang_c], axis=-1)
    ang = jnp.concatenate([ang] * (LANES // AT_HD), axis=-1)
    return jnp.cos(ang), jnp.sin(ang)


def _layer(x, norm1_g, w_in, lb_f, lb_b, onorm_g, q_norm_g, k_norm_g, w_a, w_b, w_out, norm2_g,
           w_up, conv_w, conv_b, w_down):
    B, L, D = x.shape
    c_hg = 4 * HG_W
    c_gate = 5 * HG_W
    c_at = c_gate + AT_W + 2 * AT_KV_W
    whg = w_in[:, 0:c_hg].astype(BF16)
    wat = w_in[:, c_gate:c_at].astype(BF16)
    wg = jnp.concatenate([w_in[:, c_hg:c_gate], w_in[:, c_at:]], axis=1).astype(BF16)
    n1 = norm1_g.reshape(1, D)
    n2 = norm2_g.reshape(1, D)
    qg2 = jnp.tile(q_norm_g.reshape(1, AT_HD), (1, LANES // AT_HD))
    kg2 = jnp.tile(k_norm_g.reshape(1, AT_HD), (1, LANES // AT_HD))
    cos, sin = _rope_tables(L)
    blk = np.arange(AT_W) // AT_HD
    gsum = jnp.asarray((blk[:, None] == blk[None, :]).astype(np.float32) / AT_HD, dtype=BF16)

    hq, hv, gf, gb, aq, ak, vt = _inproj(x, n1, whg, wat, lb_f, lb_b, qg2, kg2, cos, sin, gsum)
    o_f, o_b = _hgrn(hq, hv, gf, gb)
    score_bound = (AT_HD ** 0.5 * LOG2_E * 1.01) * jnp.max(jnp.abs(q_norm_g)) * jnp.max(jnp.abs(k_norm_g))
    flag = (score_bound <= SCORE_LIMIT).astype(jnp.int32).reshape(1)
    o_at = _attention(flag, aq, ak, vt)
    h, hn = _merge(x, o_f, o_b, o_at, n1, wg, onorm_g.reshape(1, HG_DK), w_a.astype(BF16),
                   w_b.astype(BF16), w_out.astype(BF16), n2)
    return _ffn(h, hn, w_up.astype(BF16), conv_w, conv_b.reshape(1, 2 * D_FF), w_down.astype(BF16))


def kernel(x, norm1_g, w_in, hg_lb_fwd, hg_lb_bwd, hg_onorm_g, q_norm_g, k_norm_g, w_branch_a,
           w_branch_b, w_out, norm2_g, w_up, conv_w, conv_b, w_down):
    depth = w_in.shape[0]
    assert depth == 1 and hg_lb_fwd.shape[0] == 2
    l = 0
    return _layer(x, norm1_g[l], w_in[l], hg_lb_fwd, hg_lb_bwd, hg_onorm_g[l], q_norm_g[l], k_norm_g[l],
                  w_branch_a[l], w_branch_b[l], w_out[l], norm2_g[l], w_up[l], conv_w[l], conv_b[l],
                  w_down[l])
```

```python
import functools

import jax
import jax.numpy as jnp
import numpy as np
from jax import lax
from jax.experimental import pallas as pl
from jax.experimental.pallas import tpu as pltpu

F32 = jnp.float32
BF16 = jnp.bfloat16

D_MODEL = 1024
GRID_W = 64
HG_HEADS = 4
HG_DK = 128
HG_W = HG_HEADS * HG_DK
HG_CHUNK = 64
AT_HEADS = 8
AT_KV_HEADS = 2
AT_GROUP = AT_HEADS // AT_KV_HEADS
AT_HD = 64
AT_W = AT_HEADS * AT_HD
AT_KV_W = AT_KV_HEADS * AT_HD
ROPE_THETA = 10000.0
ROPE_AXIS_DIM = AT_HD // 2
D_FF = 2816
EPS = 1e-6
LOG2_E = 1.4426950408889634

LANES = 128
BF16_SUBLANES = 16
VMEM_LIMIT = 56 * 1024 * 1024

VT_ROWS = AT_HD + BF16_SUBLANES
HALO = BF16_SUBLANES

TM_PROJ = 512
HG_SUB = 4 * HG_CHUNK
HG_ROWS = 512
TQ = 512
TK = 512
LOOKAHEAD = 2
LOOKAHEAD_BOUNDED = 1
SCORE_LIMIT = 60.0
KV_BLOCKS_PER_ITER = 16
TM_MERGE = 512
TM_FFN = 512
FF_CHUNK = 256


def _dot(a, b):
    return jnp.dot(a, b, preferred_element_type=F32)


def _dot_nt(a, b):
    return lax.dot_general(a, b, (((1,), (1,)), ((), ())), preferred_element_type=F32)


def _dot_tn(a, b):
    return lax.dot_general(a, b, (((0,), (0,)), ((), ())), preferred_element_type=F32)


def _sigmoid(x):
    return 1.0 / (1.0 + jnp.exp(-x))


def _rms_rows(x, g):
    ms = jnp.mean(x * x, axis=-1, keepdims=True)
    return x * lax.rsqrt(ms + EPS) * g


def _const_spec(shape):
    nd = len(shape)
    return pl.BlockSpec(shape, lambda *_: (0,) * nd)


def _lower_bound(lb_ref):
    a0 = lb_ref[0:1, :]
    a1 = lb_ref[1:2, :]
    m = jnp.maximum(a0, a1)
    e0 = jnp.exp(a0 - m)
    e1 = jnp.exp(a1 - m)
    return e0 / (e0 + e1)


def _head_rms(t, gsum):
    sq = t * t
    hi = sq.astype(BF16)
    lo = (sq - hi.astype(F32)).astype(BF16)
    ms = _dot(hi, gsum) + _dot(lo, gsum)
    return t * lax.rsqrt(ms + EPS)


def _rope(t, cos, sin_signed, first_half):
    from_hi = pltpu.roll(t, LANES - ROPE_AXIS_DIM // 2, axis=1)
    from_lo = pltpu.roll(t, ROPE_AXIS_DIM // 2, axis=1)
    return t * cos + jnp.where(first_half, from_hi, from_lo) * sin_signed


def _inproj_kernel(x_ref, n1_ref, whg_ref, wat_ref, lbf_ref, lbb_ref, qg_ref, kg_ref,
                   cos_ref, sin_ref, gsum_ref,
                   hq_ref, hv_ref, gf_ref, gb_ref, aq_ref, ak_ref, vt_ref):
    x = x_ref[0]
    tm = x.shape[0]
    u = _rms_rows(x, n1_ref[...]).astype(BF16)

    cos = cos_ref[...]
    sin = sin_ref[...]
    lane = lax.broadcasted_iota(jnp.int32, cos.shape, 1)
    first_half = (lane & (ROPE_AXIS_DIM - 1)) < (ROPE_AXIS_DIM // 2)
    sin_signed = jnp.where(first_half, -sin, sin)
    gsum = gsum_ref[...]

    def silu_q(hq):
        hq_ref[0] = hq * _sigmoid(hq)

    def values(hv):
        hv_ref[0] = hv.astype(BF16)

    def log_forget(out_ref, lb_ref):
        def epilogue(pre):
            lb = _lower_bound(lb_ref)
            out_ref[0] = jnp.log(lb + (1.0 - lb) * _sigmoid(pre))
        return epilogue

    def attn_q(aq):
        qn = _head_rms(aq, gsum)
        scale = AT_HD ** -0.5 * LOG2_E
        for c in range(AT_W // LANES):
            sl = slice(c * LANES, (c + 1) * LANES)
            t = qn[:, sl] * qg_ref[...]
            aq_ref[0, :, sl] = (_rope(t, cos, sin_signed, first_half) * scale).astype(BF16)

    def attn_k(ak):
        kn = _head_rms(ak, gsum[0:LANES, 0:LANES]) * kg_ref[...]
        kr = _rope(kn, cos, sin_signed, first_half).astype(BF16)
        for g in range(AT_KV_HEADS):
            ak_ref[0, g] = kr[:, g * AT_HD:(g + 1) * AT_HD]

    def attn_v(av):
        avt = av.T
        row = lax.broadcasted_iota(jnp.int32, (VT_ROWS - AT_HD, tm), 0)
        aug = jnp.where(row == 0, 1.0, 0.0).astype(BF16)
        for g in range(AT_KV_HEADS):
            vt_ref[0, g, 0:AT_HD, :] = avt[g * AT_HD:(g + 1) * AT_HD].astype(BF16)
            vt_ref[0, g, AT_HD:VT_ROWS, :] = aug

    sections = (
        (whg_ref, 0, HG_W, silu_q),
        (whg_ref, HG_W, HG_W, values),
        (whg_ref, 2 * HG_W, HG_W, log_forget(gf_ref, lbf_ref)),
        (whg_ref, 3 * HG_W, HG_W, log_forget(gb_ref, lbb_ref)),
        (wat_ref, 0, AT_W, attn_q),
        (wat_ref, AT_W, AT_KV_W, attn_k),
        (wat_ref, AT_W + AT_KV_W, AT_KV_W, attn_v),
    )

    def project(s):
        w_ref, off, width, _ = sections[s]
        return _dot(u, w_ref[:, off:off + width])

    nxt = project(0)
    for s in range(len(sections)):
        cur = nxt
        if s + 1 < len(sections):
            nxt = project(s + 1)
        sections[s][3](cur)


def _inproj(x, n1, whg, wat, lbf, lbb, qg2, kg2, cos, sin, gsum):
    B, L, D = x.shape
    tm = min(TM_PROJ, L)
    nt = L // tm
    tok = lambda w: pl.BlockSpec((1, tm, w), lambda b, i: (b, i, 0))
    out_shape = (
        jax.ShapeDtypeStruct((B, L, HG_W), F32),
        jax.ShapeDtypeStruct((B, L, HG_W), BF16),
        jax.ShapeDtypeStruct((B, L, HG_W), F32),
        jax.ShapeDtypeStruct((B, L, HG_W), F32),
        jax.ShapeDtypeStruct((B, L, AT_W), BF16),
        jax.ShapeDtypeStruct((B, AT_KV_HEADS, L, AT_HD), BF16),
        jax.ShapeDtypeStruct((B, AT_KV_HEADS, VT_ROWS, L), BF16),
    )
    out_specs = (
        tok(HG_W), tok(HG_W), tok(HG_W), tok(HG_W), tok(AT_W),
        pl.BlockSpec((1, AT_KV_HEADS, tm, AT_HD), lambda b, i: (b, 0, i, 0)),
        pl.BlockSpec((1, AT_KV_HEADS, VT_ROWS, tm), lambda b, i: (b, 0, 0, i)),
    )
    in_specs = [
        tok(D), _const_spec(n1.shape), _const_spec(whg.shape), _const_spec(wat.shape),
        _const_spec(lbf.shape), _const_spec(lbb.shape), _const_spec(qg2.shape), _const_spec(kg2.shape),
        pl.BlockSpec((tm, LANES), lambda b, i: (i, 0)),
        pl.BlockSpec((tm, LANES), lambda b, i: (i, 0)),
        _const_spec(gsum.shape),
    ]
    return pl.pallas_call(
        _inproj_kernel,
        grid=(B, nt),
        in_specs=in_specs,
        out_specs=out_specs,
        out_shape=out_shape,
        compiler_params=pltpu.CompilerParams(
            dimension_semantics=("arbitrary", "arbitrary"), vmem_limit_bytes=VMEM_LIMIT),
        name="inproj",
    )(x, n1, whg, wat, lbf, lbb, qg2, kg2, cos, sin, gsum)


def _chunk_mask(n, reverse):
    shift = HG_CHUNK.bit_length() - 1
    r = lax.broadcasted_iota(jnp.int32, (n, n), 0)
    c = lax.broadcasted_iota(jnp.int32, (n, n), 1)
    same = lax.shift_right_logical(r, shift) == lax.shift_right_logical(c, shift)
    return same & ((c >= r) if reverse else (c <= r))


def _split3(x):
    x1 = x.astype(BF16)
    rem = x - x1.astype(F32)
    x2 = rem.astype(BF16)
    return x1, x2, (rem - x2.astype(F32)).astype(BF16)


def _hgrn_kernel(qf_ref, vf_ref, gf_ref, qb_ref, vb_ref, gb_ref, of_ref, ob_ref, stf_ref, stb_ref):
    @pl.when(pl.program_id(1) == 0)
    def _():
        stf_ref[...] = jnp.zeros_like(stf_ref)
        stb_ref[...] = jnp.zeros_like(stb_ref)

    n = HG_SUB
    nsub = qf_ref.shape[1] // n
    nchunk = n // HG_CHUNK
    tri = {rev: _chunk_mask(n, rev) for rev in (False, True)}
    trib = {rev: jnp.where(m, 1.0, 0.0).astype(BF16) for rev, m in tri.items()}

    groups = []
    for h in range(HG_HEADS):
        cols = slice(h * HG_DK, (h + 1) * HG_DK)
        for s in range(nsub):
            groups.append(((qf_ref, vf_ref, gf_ref, of_ref), False, h, slice(s * n, (s + 1) * n), cols))
        for s in range(nsub - 1, -1, -1):
            groups.append(((qb_ref, vb_ref, gb_ref, ob_ref), True, h, slice(s * n, (s + 1) * n), cols))

    gs = [refs[2][0, rs, cs] for refs, _, _, rs, cs in groups]
    bs = []
    for g, (_, rev, _, _, _) in zip(gs, groups):
        g1, g2, g3 = _split3(g)
        bs.append(_dot(trib[rev], g1) + _dot(trib[rev], g2) + _dot(trib[rev], g3))

    qds, kds, kcs, decs = [], [], [], []
    for g, b, (refs, rev, _, rs, cs) in zip(gs, bs, groups):
        b3 = b.reshape(nchunk, HG_CHUNK, HG_DK)
        tot = b3[:, 0:1, :] if rev else b3[:, HG_CHUNK - 1:HG_CHUNK, :]
        totf = jnp.broadcast_to(tot, b3.shape).reshape(n, HG_DK)
        k = 1.0 - jnp.exp(g)
        qds.append((refs[0][0, rs, cs] * jnp.exp(b)).astype(BF16))
        kds.append((k * jnp.exp(-b)).astype(BF16))
        kcs.append((k * jnp.exp(totf - b)).astype(BF16))
        decs.append(jnp.exp(tot))

    amats = [_dot_nt(qd, kd) for qd, kd in zip(qds, kds)]
    vs = [refs[1][0, rs, cs] for refs, _, _, rs, cs in groups]
    intras = [_dot(jnp.where(tri[rev], a, 0.0).astype(BF16), v)
              for a, v, (_, rev, _, _, _) in zip(amats, vs, groups)]
    chunks = [slice(c * HG_CHUNK, (c + 1) * HG_CHUNK) for c in range(nchunk)]
    uts = [[_dot_tn(v[sl], kc[sl]) for sl in chunks] for v, kc in zip(vs, kcs)]

    states = {}
    for gi, (refs, rev, h, rs, cs) in enumerate(groups):
        st = states[(rev, h)] if (rev, h) in states else (stb_ref if rev else stf_ref)[h]
        outs = [None] * nchunk
        for ci in (range(nchunk - 1, -1, -1) if rev else range(nchunk)):
            outs[ci] = intras[gi][chunks[ci]] + _dot_nt(qds[gi][chunks[ci]], st.astype(BF16))
            st = st * decs[gi][ci] + uts[gi][ci]
        states[(rev, h)] = st
        refs[3][0, rs, cs] = jnp.concatenate(outs, axis=0)
    for (rev, h), st in states.items():
        (stb_ref if rev else stf_ref)[h] = st


def _hgrn(hq, hv, gf, gb):
    B, L, _ = hq.shape
    rows = min(HG_ROWS, L)
    nb = L // rows
    fwd = pl.BlockSpec((1, rows, HG_W), lambda b, i: (b, i, 0))
    bwd = pl.BlockSpec((1, rows, HG_W), lambda b, i: (b, nb - 1 - i, 0))
    o = jax.ShapeDtypeStruct((B, L, HG_W), F32)
    state = pltpu.VMEM((HG_HEADS, HG_DK, HG_DK), F32)
    return pl.pallas_call(
        _hgrn_kernel,
        grid=(B, nb),
        in_specs=[fwd, fwd, fwd, bwd, bwd, bwd],
        out_specs=(fwd, bwd),
        out_shape=(o, o),
        scratch_shapes=[state, state],
        compiler_params=pltpu.CompilerParams(
            dimension_semantics=("arbitrary", "arbitrary"), vmem_limit_bytes=VMEM_LIMIT),
        name="hgrn2",
    )(hq, hv, gf, hq, hv, gb)


def _attn_kernel(flag_ref, q_ref, k_ref, vt_ref, o_ref, acc_ref, *, tk, inner):
    tq = q_ref.shape[1]
    nk = k_ref.shape[2] // tk
    qs = [q_ref[0, :, j * AT_HD:(j + 1) * AT_HD] for j in range(AT_GROUP)]

    def scores(blk, j, stabilised):
        start = pl.multiple_of(blk * tk, tk)
        st = _dot_nt(k_ref[0, 0, pl.ds(start, tk), :], qs[j])
        return st if stabilised else jnp.exp2(st).astype(BF16)

    def run(stabilised, lookahead):
        def body(it, carry):
            pending, state = carry
            pending = list(pending)
            state = list(state)
            for n in range(inner * AT_GROUP):
                blk = it * inner + n // AT_GROUP
                j = n % AT_GROUP
                start = pl.multiple_of(blk * tk, tk)
                vb = vt_ref[0, 0, :, pl.ds(start, tk)]
                st = pending.pop(0)
                ahead = n + lookahead
                pending.append(scores(jnp.minimum(it * inner + ahead // AT_GROUP, nk - 1), ahead % AT_GROUP,
                                      stabilised))
                if stabilised:
                    m, acc = state[j]
                    m_new = jnp.maximum(m, jnp.max(st, axis=0, keepdims=True))
                    p = jnp.exp2(st - m_new).astype(BF16)
                    state[j] = (m_new, jnp.exp2(m - m_new) * acc + _dot(vb, p))
                else:
                    state[j] = state[j] + _dot(vb, st)
            return tuple(pending), tuple(state)

        zero = jnp.zeros((VT_ROWS, tq), F32)
        init = tuple((jnp.full((1, tq), -1e30, F32), zero) if stabilised else zero for _ in range(AT_GROUP))
        first = tuple(scores(min(n // AT_GROUP, nk - 1), n % AT_GROUP, stabilised) for n in range(lookahead))
        _, final = lax.fori_loop(0, nk // inner, body, (first, init))
        for j in range(AT_GROUP):
            acc_ref[j] = final[j][1] if stabilised else final[j]

    @pl.when(flag_ref[0] != 0)
    def _():
        run(False, LOOKAHEAD_BOUNDED)

    @pl.when(flag_ref[0] == 0)
    def _():
        run(True, LOOKAHEAD)

    normed = [acc_ref[j, 0:AT_HD, :] / acc_ref[j, AT_HD:AT_HD + 1, :] for j in range(AT_GROUP)]
    pairs = [jnp.concatenate(normed[2 * p:2 * p + 2], axis=0).T for p in range(AT_GROUP // 2)]
    o_ref[0] = jnp.concatenate(pairs, axis=1).astype(o_ref.dtype)


def _attention(flag, aq, ak, vt):
    B, L, _ = aq.shape
    tq = min(TQ, L)
    tk = min(TK, L)
    gw = AT_GROUP * AT_HD
    return pl.pallas_call(
        functools.partial(_attn_kernel, tk=tk, inner=min(KV_BLOCKS_PER_ITER, L // tk)),
        grid=(B, AT_KV_HEADS, L // tq),
        in_specs=[
            pl.BlockSpec(memory_space=pltpu.SMEM),
            pl.BlockSpec((1, tq, gw), lambda b, g, i: (b, i, g)),
            pl.BlockSpec((1, 1, L, AT_HD), lambda b, g, i: (b, g, 0, 0)),
            pl.BlockSpec((1, 1, VT_ROWS, L), lambda b, g, i: (b, g, 0, 0)),
        ],
        out_specs=pl.BlockSpec((1, tq, gw), lambda b, g, i: (b, i, g)),
        out_shape=jax.ShapeDtypeStruct((B, L, AT_W), BF16),
        scratch_shapes=[pltpu.VMEM((AT_GROUP, VT_ROWS, tq), F32)],
        compiler_params=pltpu.CompilerParams(
            dimension_semantics=("arbitrary", "arbitrary", "arbitrary"), vmem_limit_bytes=VMEM_LIMIT),
        name="attention",
    )(flag, aq, ak, vt)


def _merge_kernel(x_ref, of_ref, ob_ref, oat_ref, n1_ref, wgh_ref, wgab_ref, ong_ref, wa_ref, wb_ref,
                  wo_ref, n2_ref, h_ref, hn_ref):
    x = x_ref[0]
    u = _rms_rows(x, n1_ref[...]).astype(BF16)
    hgate = _dot(u, wgh_ref[...])
    yb = _dot(oat_ref[0], wb_ref[...])
    gb = _dot(u, wgab_ref[:, D_MODEL:2 * D_MODEL])
    o = of_ref[0] + ob_ref[0]
    heads = []
    for h in range(HG_HEADS):
        sl = slice(h * HG_DK, (h + 1) * HG_DK)
        heads.append(_rms_rows(o[:, sl], ong_ref[...]))
    oa = (jnp.concatenate(heads, axis=1) * (hgate * _sigmoid(hgate))).astype(BF16)
    ga = _dot(u, wgab_ref[:, 0:D_MODEL])
    gated_b = _sigmoid(gb) * yb
    ya = _dot(oa, wa_ref[...])
    merged = (_sigmoid(ga) * ya + gated_b).astype(BF16)
    h = x + _dot(merged, wo_ref[...])
    h_ref[0] = h
    hn_ref[0] = _rms_rows(h, n2_ref[...]).astype(BF16)


def _merge(x, o_f, o_b, o_at, n1, wgh, wgab, ong, wa, wb, wo, n2):
    B, L, D = x.shape
    tm = min(TM_MERGE, L)
    tok = lambda w: pl.BlockSpec((1, tm, w), lambda b, i: (b, i, 0))
    return pl.pallas_call(
        _merge_kernel,
        grid=(B, L // tm),
        in_specs=[tok(D), tok(HG_W), tok(HG_W), tok(AT_W), _const_spec(n1.shape), _const_spec(wgh.shape),
                  _const_spec(wgab.shape), _const_spec(ong.shape), _const_spec(wa.shape),
                  _const_spec(wb.shape), _const_spec(wo.shape), _const_spec(n2.shape)],
        out_specs=(tok(D), tok(D)),
        out_shape=(jax.ShapeDtypeStruct((B, L, D), F32), jax.ShapeDtypeStruct((B, L, D), BF16)),
        compiler_params=pltpu.CompilerParams(
            dimension_semantics=("arbitrary", "arbitrary"), vmem_limit_bytes=VMEM_LIMIT),
        name="merge",
    )(x, o_f, o_b, o_at, n1, wgh, wgab, ong, wa, wb, wo, n2)


def _ffn_kernel(h_ref, hn_ref, prev_ref, next_ref, wup_ref, cw_ref, cb_ref, wdn_ref, y_ref, ext_ref):
    i = pl.program_id(1)
    nt = pl.num_programs(1)
    tm = hn_ref.shape[1]
    ext = tm + 2 * HALO
    ext_ref[0:HALO, :] = jnp.where(i > 0, prev_ref[0], jnp.zeros_like(prev_ref[0]))
    ext_ref[HALO:HALO + tm, :] = hn_ref[0]
    ext_ref[HALO + tm:ext, :] = jnp.where(i < nt - 1, next_ref[0], jnp.zeros_like(next_ref[0]))
    hx = ext_ref[...]

    def conv(t, off):
        w = cw_ref[:, off:off + FF_CHUNK]
        before = pltpu.roll(t, 1, axis=0)[HALO:HALO + tm]
        after = pltpu.roll(t, ext - 1, axis=0)[HALO:HALO + tm]
        return (before * w[0:1] + t[HALO:HALO + tm] * w[1:2] + after * w[2:3]
                + cb_ref[:, off:off + FF_CHUNK])

    def up(j):
        off = j * FF_CHUNK
        return (_dot(hx, wup_ref[:, off:off + FF_CHUNK]),
                _dot(hx, wup_ref[:, D_FF + off:D_FF + off + FF_CHUNK]))

    nchunk = D_FF // FF_CHUNK
    acc = jnp.zeros((tm, D_MODEL), F32)
    nxt = up(0)
    for j in range(nchunk):
        off = j * FF_CHUNK
        upv, upg = nxt
        if j + 1 < nchunk:
            nxt = up(j + 1)
        val = conv(upv, off)
        gate = conv(upg, D_FF + off)
        act = (gate * _sigmoid(gate) * val).astype(BF16)
        acc = acc + _dot(act, wdn_ref[off:off + FF_CHUNK, :])
    y_ref[0] = h_ref[0] + acc


def _ffn(h, hn, wup, cw, cb, wdn):
    B, L, D = h.shape
    tm = min(TM_FFN, L)
    per = tm // HALO
    nh = L // HALO
    tok = pl.BlockSpec((1, tm, D), lambda b, i: (b, i, 0))
    return pl.pallas_call(
        _ffn_kernel,
        grid=(B, L // tm),
        in_specs=[
            tok, tok,
            pl.BlockSpec((1, HALO, D), lambda b, i: (b, jnp.maximum(i * per - 1, 0), 0)),
            pl.BlockSpec((1, HALO, D), lambda b, i: (b, jnp.minimum((i + 1) * per, nh - 1), 0)),
            _const_spec(wup.shape), _const_spec(cw.shape), _const_spec(cb.shape), _const_spec(wdn.shape),
        ],
        out_specs=tok,
        out_shape=jax.ShapeDtypeStruct((B, L, D), F32),
        scratch_shapes=[pltpu.VMEM((tm + 2 * HALO, D), BF16)],
        compiler_params=pltpu.CompilerParams(
            dimension_semantics=("arbitrary", "arbitrary"), vmem_limit_bytes=VMEM_LIMIT),
        name="convffn",
    )(h, hn, hn, hn, wup, cw, cb, wdn)


def _rope_tables(L):
    rows = L // GRID_W
    inv = ROPE_THETA ** (-jnp.arange(0, ROPE_AXIS_DIM, 2, dtype=F32) / ROPE_AXIS_DIM)
    ang_r = jnp.arange(rows, dtype=F32)[:, None] * inv[None, :]
    ang_c = jnp.arange(GRID_W, dtype=F32)[:, None] * inv[None, :]

    def table(fn):
        r = jnp.broadcast_to(fn(ang_r)[:, None, :], (rows, GRID_W, inv.shape[0])).reshape(L, -1)
        c = jnp.broadcast_to(fn(ang_c)[None, :, :], (rows, GRID_W, inv.shape[0])).reshape(L, -1)
        return jnp.concatenate([r, r, c, c] * (LANES // AT_HD), axis=-1)

    return table(jnp.cos), table(jnp.sin)


def _layer(x, norm1_g, w_in, lb_f, lb_b, onorm_g, q_norm_g, k_norm_g, w_a, w_b, w_out, norm2_g,
           w_up, conv_w, conv_b, w_down):
    B, L, D = x.shape
    c_hg = 4 * HG_W
    c_gate = 5 * HG_W
    c_at = c_gate + AT_W + 2 * AT_KV_W
    whg = w_in[:, 0:c_hg].astype(BF16)
    wat = w_in[:, c_gate:c_at].astype(BF16)
    wgh = w_in[:, c_hg:c_gate].astype(BF16)
    wgab = w_in[:, c_at:].astype(BF16)
    n1 = norm1_g.reshape(1, D)
    n2 = norm2_g.reshape(1, D)
    qg2 = jnp.tile(q_norm_g.reshape(1, AT_HD), (1, LANES // AT_HD))
    kg2 = jnp.tile(k_norm_g.reshape(1, AT_HD), (1, LANES // AT_HD))
    cos, sin = _rope_tables(L)
    blk = np.arange(AT_W) // AT_HD
    gsum = jnp.asarray((blk[:, None] == blk[None, :]).astype(np.float32) / AT_HD, dtype=BF16)

    hq, hv, gf, gb, aq, ak, vt = _inproj(x, n1, whg, wat, lb_f, lb_b, qg2, kg2, cos, sin, gsum)
    o_f, o_b = _hgrn(hq, hv, gf, gb)
    score_bound = (AT_HD ** 0.5 * LOG2_E * 1.01) * jnp.max(jnp.abs(q_norm_g)) * jnp.max(jnp.abs(k_norm_g))
    flag = (score_bound <= SCORE_LIMIT).astype(jnp.int32).reshape(1)
    o_at = _attention(flag, aq, ak, vt)
    h, hn = _merge(x, o_f, o_b, o_at, n1, wgh, wgab, onorm_g.reshape(1, HG_DK), w_a.astype(BF16),
                   w_b.astype(BF16), w_out.astype(BF16), n2)
    return _ffn(h, hn, w_up.astype(BF16), conv_w, conv_b.reshape(1, 2 * D_FF), w_down.astype(BF16))


def kernel(x, norm1_g, w_in, hg_lb_fwd, hg_lb_bwd, hg_onorm_g, q_norm_g, k_norm_g, w_branch_a,
           w_branch_b, w_out, norm2_g, w_up, conv_w, conv_b, w_down):
    depth = w_in.shape[0]
    assert depth == 1 and hg_lb_fwd.shape[0] == 2
    l = 0
    return _layer(x, norm1_g[l], w_in[l], hg_lb_fwd, hg_lb_bwd, hg_onorm_g[l], q_norm_g[l], k_norm_g[l],
                  w_branch_a[l], w_branch_b[l], w_out[l], norm2_g[l], w_up[l], conv_w[l], conv_b[l],
                  w_down[l])
```

```python
import functools

import jax
import jax.numpy as jnp
import numpy as np
from jax import lax
from jax.experimental import pallas as pl
from jax.experimental.pallas import tpu as pltpu

F32 = jnp.float32
BF16 = jnp.bfloat16

D_MODEL = 1024
GRID_W = 64
HG_HEADS = 4
HG_DK = 128
HG_W = HG_HEADS * HG_DK
HG_CHUNK = 64
AT_HEADS = 8
AT_KV_HEADS = 2
AT_GROUP = AT_HEADS // AT_KV_HEADS
AT_HD = 64
AT_W = AT_HEADS * AT_HD
AT_KV_W = AT_KV_HEADS * AT_HD
ROPE_THETA = 10000.0
ROPE_AXIS_DIM = AT_HD // 2
D_FF = 2816
EPS = 1e-6
LOG2_E = 1.4426950408889634

LANES = 128
BF16_SUBLANES = 16
VMEM_LIMIT = 56 * 1024 * 1024

VT_ROWS = AT_HD + BF16_SUBLANES
HALO = BF16_SUBLANES

TM_PROJ = 512
HG_SUB = 4 * HG_CHUNK
HG_ROWS = 512
TQ = 512
TK = 512
LOOKAHEAD = 2
LOOKAHEAD_BOUNDED = 1
SCORE_LIMIT = 60.0
KV_BLOCKS_PER_ITER = 16
TM_MERGE = 512
TM_FFN = 512
FF_CHUNK = 256
FF_DOWN_GROUPS = 1


def _dot(a, b):
    return jnp.dot(a, b, preferred_element_type=F32)


def _dot_nt(a, b):
    return lax.dot_general(a, b, (((1,), (1,)), ((), ())), preferred_element_type=F32)


def _dot_tn(a, b):
    return lax.dot_general(a, b, (((0,), (0,)), ((), ())), preferred_element_type=F32)


def _sigmoid(x):
    return 1.0 / (1.0 + jnp.exp(-x))


def _rms_rows(x, g):
    ms = jnp.mean(x * x, axis=-1, keepdims=True)
    return x * lax.rsqrt(ms + EPS) * g


def _const_spec(shape):
    nd = len(shape)
    return pl.BlockSpec(shape, lambda *_: (0,) * nd)


def _lower_bound(lb_ref):
    a0 = lb_ref[0:1, :]
    a1 = lb_ref[1:2, :]
    m = jnp.maximum(a0, a1)
    e0 = jnp.exp(a0 - m)
    e1 = jnp.exp(a1 - m)
    return e0 / (e0 + e1)


def _head_rms(t, gsum):
    sq = t * t
    hi = sq.astype(BF16)
    lo = (sq - hi.astype(F32)).astype(BF16)
    ms = _dot(hi, gsum) + _dot(lo, gsum)
    return t * lax.rsqrt(ms + EPS)


def _rope(t, cos, sin_signed, first_half):
    from_hi = pltpu.roll(t, LANES - ROPE_AXIS_DIM // 2, axis=1)
    from_lo = pltpu.roll(t, ROPE_AXIS_DIM // 2, axis=1)
    return t * cos + jnp.where(first_half, from_hi, from_lo) * sin_signed


def _inproj_kernel(x_ref, n1_ref, whg_ref, wat_ref, lbf_ref, lbb_ref, qg_ref, kg_ref,
                   cos_ref, sin_ref, gsum_ref,
                   hq_ref, hv_ref, gf_ref, gb_ref, aq_ref, ak_ref, vt_ref):
    x = x_ref[0]
    tm = x.shape[0]
    u = _rms_rows(x, n1_ref[...]).astype(BF16)

    cos = cos_ref[...]
    sin = sin_ref[...]
    lane = lax.broadcasted_iota(jnp.int32, cos.shape, 1)
    first_half = (lane & (ROPE_AXIS_DIM - 1)) < (ROPE_AXIS_DIM // 2)
    sin_signed = jnp.where(first_half, -sin, sin)
    gsum = gsum_ref[...]

    def silu_q(hq):
        hq_ref[0] = hq * _sigmoid(hq)

    def values(hv):
        hv_ref[0] = hv.astype(BF16)

    def log_forget(out_ref, lb_ref):
        def epilogue(pre):
            lb = _lower_bound(lb_ref)
            out_ref[0] = jnp.log(lb + (1.0 - lb) * _sigmoid(pre))
        return epilogue

    def attn_q(aq):
        qn = _head_rms(aq, gsum)
        scale = AT_HD ** -0.5 * LOG2_E
        for c in range(AT_W // LANES):
            sl = slice(c * LANES, (c + 1) * LANES)
            t = qn[:, sl] * qg_ref[...]
            aq_ref[0, :, sl] = (_rope(t, cos, sin_signed, first_half) * scale).astype(BF16)

    def attn_k(ak):
        kn = _head_rms(ak, gsum[0:LANES, 0:LANES]) * kg_ref[...]
        kr = _rope(kn, cos, sin_signed, first_half).astype(BF16)
        for g in range(AT_KV_HEADS):
            ak_ref[0, g] = kr[:, g * AT_HD:(g + 1) * AT_HD]

    def attn_v(av):
        avt = av.T
        row = lax.broadcasted_iota(jnp.int32, (VT_ROWS - AT_HD, tm), 0)
        aug = jnp.where(row == 0, 1.0, 0.0).astype(BF16)
        for g in range(AT_KV_HEADS):
            vt_ref[0, g, 0:AT_HD, :] = avt[g * AT_HD:(g + 1) * AT_HD].astype(BF16)
            vt_ref[0, g, AT_HD:VT_ROWS, :] = aug

    sections = (
        (whg_ref, 0, HG_W, silu_q),
        (whg_ref, HG_W, HG_W, values),
        (whg_ref, 2 * HG_W, HG_W, log_forget(gf_ref, lbf_ref)),
        (whg_ref, 3 * HG_W, HG_W, log_forget(gb_ref, lbb_ref)),
        (wat_ref, 0, AT_W, attn_q),
        (wat_ref, AT_W, AT_KV_W, attn_k),
        (wat_ref, AT_W + AT_KV_W, AT_KV_W, attn_v),
    )

    def project(s):
        w_ref, off, width, _ = sections[s]
        return _dot(u, w_ref[:, off:off + width])

    nxt = project(0)
    for s in range(len(sections)):
        cur = nxt
        if s + 1 < len(sections):
            nxt = project(s + 1)
        sections[s][3](cur)


def _inproj(x, n1, whg, wat, lbf, lbb, qg2, kg2, cos, sin, gsum):
    B, L, D = x.shape
    tm = min(TM_PROJ, L)
    nt = L // tm
    tok = lambda w: pl.BlockSpec((1, tm, w), lambda b, i: (b, i, 0))
    out_shape = (
        jax.ShapeDtypeStruct((B, L, HG_W), F32),
        jax.ShapeDtypeStruct((B, L, HG_W), BF16),
        jax.ShapeDtypeStruct((B, L, HG_W), F32),
        jax.ShapeDtypeStruct((B, L, HG_W), F32),
        jax.ShapeDtypeStruct((B, L, AT_W), BF16),
        jax.ShapeDtypeStruct((B, AT_KV_HEADS, L, AT_HD), BF16),
        jax.ShapeDtypeStruct((B, AT_KV_HEADS, VT_ROWS, L), BF16),
    )
    out_specs = (
        tok(HG_W), tok(HG_W), tok(HG_W), tok(HG_W), tok(AT_W),
        pl.BlockSpec((1, AT_KV_HEADS, tm, AT_HD), lambda b, i: (b, 0, i, 0)),
        pl.BlockSpec((1, AT_KV_HEADS, VT_ROWS, tm), lambda b, i: (b, 0, 0, i)),
    )
    in_specs = [
        tok(D), _const_spec(n1.shape), _const_spec(whg.shape), _const_spec(wat.shape),
        _const_spec(lbf.shape), _const_spec(lbb.shape), _const_spec(qg2.shape), _const_spec(kg2.shape),
        pl.BlockSpec((tm, LANES), lambda b, i: (i, 0)),
        pl.BlockSpec((tm, LANES), lambda b, i: (i, 0)),
        _const_spec(gsum.shape),
    ]
    return pl.pallas_call(
        _inproj_kernel,
        grid=(B, nt),
        in_specs=in_specs,
        out_specs=out_specs,
        out_shape=out_shape,
        compiler_params=pltpu.CompilerParams(
            dimension_semantics=("arbitrary", "arbitrary"), vmem_limit_bytes=VMEM_LIMIT),
        name="inproj",
    )(x, n1, whg, wat, lbf, lbb, qg2, kg2, cos, sin, gsum)


def _chunk_mask(n, reverse):
    shift = HG_CHUNK.bit_length() - 1
    r = lax.broadcasted_iota(jnp.int32, (n, n), 0)
    c = lax.broadcasted_iota(jnp.int32, (n, n), 1)
    same = lax.shift_right_logical(r, shift) == lax.shift_right_logical(c, shift)
    return same & ((c >= r) if reverse else (c <= r))


def _split3(x):
    x1 = x.astype(BF16)
    rem = x - x1.astype(F32)
    x2 = rem.astype(BF16)
    return x1, x2, (rem - x2.astype(F32)).astype(BF16)


def _hgrn_kernel(qf_ref, vf_ref, gf_ref, qb_ref, vb_ref, gb_ref, of_ref, ob_ref, stf_ref, stb_ref):
    @pl.when(pl.program_id(1) == 0)
    def _():
        stf_ref[...] = jnp.zeros_like(stf_ref)
        stb_ref[...] = jnp.zeros_like(stb_ref)

    n = HG_SUB
    nsub = qf_ref.shape[1] // n
    nchunk = n // HG_CHUNK
    tri = {rev: _chunk_mask(n, rev) for rev in (False, True)}
    trib = {rev: jnp.where(m, 1.0, 0.0).astype(BF16) for rev, m in tri.items()}

    groups = []
    for h in range(HG_HEADS):
        cols = slice(h * HG_DK, (h + 1) * HG_DK)
        for s in range(nsub):
            groups.append(((qf_ref, vf_ref, gf_ref, of_ref), False, h, slice(s * n, (s + 1) * n), cols))
        for s in range(nsub - 1, -1, -1):
            groups.append(((qb_ref, vb_ref, gb_ref, ob_ref), True, h, slice(s * n, (s + 1) * n), cols))

    gs = [refs[2][0, rs, cs] for refs, _, _, rs, cs in groups]
    bs = []
    for g, (_, rev, _, _, _) in zip(gs, groups):
        g1, g2, g3 = _split3(g)
        bs.append(_dot(trib[rev], g1) + _dot(trib[rev], g2) + _dot(trib[rev], g3))

    qds, kds, kcs, decs = [], [], [], []
    for g, b, (refs, rev, _, rs, cs) in zip(gs, bs, groups):
        b3 = b.reshape(nchunk, HG_CHUNK, HG_DK)
        tot = b3[:, 0:1, :] if rev else b3[:, HG_CHUNK - 1:HG_CHUNK, :]
        totf = jnp.broadcast_to(tot, b3.shape).reshape(n, HG_DK)
        k = 1.0 - jnp.exp(g)
        qds.append((refs[0][0, rs, cs] * jnp.exp(b)).astype(BF16))
        kds.append((k * jnp.exp(-b)).astype(BF16))
        kcs.append((k * jnp.exp(totf - b)).astype(BF16))
        decs.append(jnp.exp(tot))

    amats = [_dot_nt(qd, kd) for qd, kd in zip(qds, kds)]
    vs = [refs[1][0, rs, cs] for refs, _, _, rs, cs in groups]
    intras = [_dot(jnp.where(tri[rev], a, 0.0).astype(BF16), v)
              for a, v, (_, rev, _, _, _) in zip(amats, vs, groups)]
    chunks = [slice(c * HG_CHUNK, (c + 1) * HG_CHUNK) for c in range(nchunk)]
    uts = [[_dot_tn(v[sl], kc[sl]) for sl in chunks] for v, kc in zip(vs, kcs)]

    states = {}
    for gi, (refs, rev, h, rs, cs) in enumerate(groups):
        st = states[(rev, h)] if (rev, h) in states else (stb_ref if rev else stf_ref)[h]
        outs = [None] * nchunk
        for ci in (range(nchunk - 1, -1, -1) if rev else range(nchunk)):
            outs[ci] = intras[gi][chunks[ci]] + _dot_nt(qds[gi][chunks[ci]], st.astype(BF16))
            st = st * decs[gi][ci] + uts[gi][ci]
        states[(rev, h)] = st
        refs[3][0, rs, cs] = jnp.concatenate(outs, axis=0)
    for (rev, h), st in states.items():
        (stb_ref if rev else stf_ref)[h] = st


def _hgrn(hq, hv, gf, gb):
    B, L, _ = hq.shape
    rows = min(HG_ROWS, L)
    nb = L // rows
    fwd = pl.BlockSpec((1, rows, HG_W), lambda b, i: (b, i, 0))
    bwd = pl.BlockSpec((1, rows, HG_W), lambda b, i: (b, nb - 1 - i, 0))
    o = jax.ShapeDtypeStruct((B, L, HG_W), F32)
    state = pltpu.VMEM((HG_HEADS, HG_DK, HG_DK), F32)
    return pl.pallas_call(
        _hgrn_kernel,
        grid=(B, nb),
        in_specs=[fwd, fwd, fwd, bwd, bwd, bwd],
        out_specs=(fwd, bwd),
        out_shape=(o, o),
        scratch_shapes=[state, state],
        compiler_params=pltpu.CompilerParams(
            dimension_semantics=("arbitrary", "arbitrary"), vmem_limit_bytes=VMEM_LIMIT),
        name="hgrn2",
    )(hq, hv, gf, hq, hv, gb)


def _attn_kernel(flag_ref, q_ref, k_ref, vt_ref, o_ref, acc_ref, *, tk, inner):
    tq = q_ref.shape[1]
    nk = k_ref.shape[2] // tk
    qs = [q_ref[0, :, j * AT_HD:(j + 1) * AT_HD] for j in range(AT_GROUP)]

    def scores(blk, j, stabilised):
        start = pl.multiple_of(blk * tk, tk)
        st = _dot_nt(k_ref[0, 0, pl.ds(start, tk), :], qs[j])
        return st if stabilised else jnp.exp2(st).astype(BF16)

    def run(stabilised, lookahead):
        def body(it, carry):
            pending, state = carry
            pending = list(pending)
            state = list(state)
            for n in range(inner * AT_GROUP):
                blk = it * inner + n // AT_GROUP
                j = n % AT_GROUP
                start = pl.multiple_of(blk * tk, tk)
                vb = vt_ref[0, 0, :, pl.ds(start, tk)]
                st = pending.pop(0)
                ahead = n + lookahead
                pending.append(scores(jnp.minimum(it * inner + ahead // AT_GROUP, nk - 1), ahead % AT_GROUP,
                                      stabilised))
                if stabilised:
                    m, acc = state[j]
                    m_new = jnp.maximum(m, jnp.max(st, axis=0, keepdims=True))
                    p = jnp.exp2(st - m_new).astype(BF16)
                    state[j] = (m_new, jnp.exp2(m - m_new) * acc + _dot(vb, p))
                else:
                    state[j] = state[j] + _dot(vb, st)
            return tuple(pending), tuple(state)

        zero = jnp.zeros((VT_ROWS, tq), F32)
        init = tuple((jnp.full((1, tq), -1e30, F32), zero) if stabilised else zero for _ in range(AT_GROUP))
        first = tuple(scores(min(n // AT_GROUP, nk - 1), n % AT_GROUP, stabilised) for n in range(lookahead))
        _, final = lax.fori_loop(0, nk // inner, body, (first, init))
        for j in range(AT_GROUP):
            acc_ref[j] = final[j][1] if stabilised else final[j]

    @pl.when(flag_ref[0] != 0)
    def _():
        run(False, LOOKAHEAD_BOUNDED)

    @pl.when(flag_ref[0] == 0)
    def _():
        run(True, LOOKAHEAD)

    normed = [acc_ref[j, 0:AT_HD, :] / acc_ref[j, AT_HD:AT_HD + 1, :] for j in range(AT_GROUP)]
    pairs = [jnp.concatenate(normed[2 * p:2 * p + 2], axis=0).T for p in range(AT_GROUP // 2)]
    o_ref[0] = jnp.concatenate(pairs, axis=1).astype(o_ref.dtype)


def _attention(flag, aq, ak, vt):
    B, L, _ = aq.shape
    tq = min(TQ, L)
    tk = min(TK, L)
    gw = AT_GROUP * AT_HD
    return pl.pallas_call(
        functools.partial(_attn_kernel, tk=tk, inner=min(KV_BLOCKS_PER_ITER, L // tk)),
        grid=(B, AT_KV_HEADS, L // tq),
        in_specs=[
            pl.BlockSpec(memory_space=pltpu.SMEM),
            pl.BlockSpec((1, tq, gw), lambda b, g, i: (b, i, g)),
            pl.BlockSpec((1, 1, L, AT_HD), lambda b, g, i: (b, g, 0, 0)),
            pl.BlockSpec((1, 1, VT_ROWS, L), lambda b, g, i: (b, g, 0, 0)),
        ],
        out_specs=pl.BlockSpec((1, tq, gw), lambda b, g, i: (b, i, g)),
        out_shape=jax.ShapeDtypeStruct((B, L, AT_W), BF16),
        scratch_shapes=[pltpu.VMEM((AT_GROUP, VT_ROWS, tq), F32)],
        compiler_params=pltpu.CompilerParams(
            dimension_semantics=("arbitrary", "arbitrary", "arbitrary"), vmem_limit_bytes=VMEM_LIMIT),
        name="attention",
    )(flag, aq, ak, vt)


def _merge_kernel(x_ref, of_ref, ob_ref, oat_ref, n1_ref, wgh_ref, wgab_ref, ong_ref, wa_ref, wb_ref,
                  wo_ref, n2_ref, h_ref, hn_ref):
    x = x_ref[0]
    u = _rms_rows(x, n1_ref[...]).astype(BF16)
    hgate = _dot(u, wgh_ref[...])
    yb = _dot(oat_ref[0], wb_ref[...])
    gb = _dot(u, wgab_ref[:, D_MODEL:2 * D_MODEL])
    o = of_ref[0] + ob_ref[0]
    heads = []
    for h in range(HG_HEADS):
        sl = slice(h * HG_DK, (h + 1) * HG_DK)
        heads.append(_rms_rows(o[:, sl], ong_ref[...]))
    oa = (jnp.concatenate(heads, axis=1) * (hgate * _sigmoid(hgate))).astype(BF16)
    ga = _dot(u, wgab_ref[:, 0:D_MODEL])
    gated_b = _sigmoid(gb) * yb
    ya = _dot(oa, wa_ref[...])
    merged = (_sigmoid(ga) * ya + gated_b).astype(BF16)
    h = x + _dot(merged, wo_ref[...])
    h_ref[0] = h
    hn_ref[0] = _rms_rows(h, n2_ref[...]).astype(BF16)


def _merge(x, o_f, o_b, o_at, n1, wgh, wgab, ong, wa, wb, wo, n2):
    B, L, D = x.shape
    tm = min(TM_MERGE, L)
    tok = lambda w: pl.BlockSpec((1, tm, w), lambda b, i: (b, i, 0))
    return pl.pallas_call(
        _merge_kernel,
        grid=(B, L // tm),
        in_specs=[tok(D), tok(HG_W), tok(HG_W), tok(AT_W), _const_spec(n1.shape), _const_spec(wgh.shape),
                  _const_spec(wgab.shape), _const_spec(ong.shape), _const_spec(wa.shape),
                  _const_spec(wb.shape), _const_spec(wo.shape), _const_spec(n2.shape)],
        out_specs=(tok(D), tok(D)),
        out_shape=(jax.ShapeDtypeStruct((B, L, D), F32), jax.ShapeDtypeStruct((B, L, D), BF16)),
        compiler_params=pltpu.CompilerParams(
            dimension_semantics=("arbitrary", "arbitrary"), vmem_limit_bytes=VMEM_LIMIT),
        name="merge",
    )(x, o_f, o_b, o_at, n1, wgh, wgab, ong, wa, wb, wo, n2)


def _ffn_kernel(h_ref, hn_ref, prev_ref, next_ref, wup_ref, cw_ref, cb_ref, wdn_ref, y_ref, ext_ref,
                act_ref):
    i = pl.program_id(1)
    nt = pl.num_programs(1)
    tm = hn_ref.shape[1]
    ext = tm + 2 * HALO
    ext_ref[0:HALO, :] = jnp.where(i > 0, prev_ref[0], jnp.zeros_like(prev_ref[0]))
    ext_ref[HALO:HALO + tm, :] = hn_ref[0]
    ext_ref[HALO + tm:ext, :] = jnp.where(i < nt - 1, next_ref[0], jnp.zeros_like(next_ref[0]))
    hx = ext_ref[...]

    def conv(t, off):
        w = cw_ref[:, off:off + FF_CHUNK]
        before = pltpu.roll(t, 1, axis=0)[HALO:HALO + tm]
        after = pltpu.roll(t, ext - 1, axis=0)[HALO:HALO + tm]
        return (before * w[0:1] + t[HALO:HALO + tm] * w[1:2] + after * w[2:3]
                + cb_ref[:, off:off + FF_CHUNK])

    def up(j):
        off = j * FF_CHUNK
        return (_dot(hx, wup_ref[:, off:off + FF_CHUNK]),
                _dot(hx, wup_ref[:, D_FF + off:D_FF + off + FF_CHUNK]))

    nchunk = D_FF // FF_CHUNK
    bounds = [round(g * nchunk / FF_DOWN_GROUPS) * FF_CHUNK for g in range(FF_DOWN_GROUPS + 1)]
    y = h_ref[0]
    nxt = up(0)
    for j in range(nchunk):
        off = j * FF_CHUNK
        upv, upg = nxt
        if j + 1 < nchunk:
            nxt = up(j + 1)
        val = conv(upv, off)
        gate = conv(upg, D_FF + off)
        act_ref[:, off:off + FF_CHUNK] = (gate * _sigmoid(gate) * val).astype(BF16)
        if off + FF_CHUNK in bounds[1:]:
            lo = bounds[bounds.index(off + FF_CHUNK) - 1]
            y = y + _dot(act_ref[:, lo:off + FF_CHUNK], wdn_ref[lo:off + FF_CHUNK, :])
    y_ref[0] = y


def _ffn(h, hn, wup, cw, cb, wdn):
    B, L, D = h.shape
    tm = min(TM_FFN, L)
    per = tm // HALO
    nh = L // HALO
    tok = pl.BlockSpec((1, tm, D), lambda b, i: (b, i, 0))
    return pl.pallas_call(
        _ffn_kernel,
        grid=(B, L // tm),
        in_specs=[
            tok, tok,
            pl.BlockSpec((1, HALO, D), lambda b, i: (b, jnp.maximum(i * per - 1, 0), 0)),
            pl.BlockSpec((1, HALO, D), lambda b, i: (b, jnp.minimum((i + 1) * per, nh - 1), 0)),
            _const_spec(wup.shape), _const_spec(cw.shape), _const_spec(cb.shape), _const_spec(wdn.shape),
        ],
        out_specs=tok,
        out_shape=jax.ShapeDtypeStruct((B, L, D), F32),
        scratch_shapes=[pltpu.VMEM((tm + 2 * HALO, D), BF16), pltpu.VMEM((tm, D_FF), BF16)],
        compiler_params=pltpu.CompilerParams(
            dimension_semantics=("arbitrary", "arbitrary"), vmem_limit_bytes=VMEM_LIMIT),
        name="convffn",
    )(h, hn, hn, hn, wup, cw, cb, wdn)


def _rope_tables(L):
    pos = np.arange(L)
    inv = ROPE_THETA ** (-np.arange(0, ROPE_AXIS_DIM, 2, dtype=np.float64) / ROPE_AXIS_DIM)
    ang_r = (pos // GRID_W)[:, None] * inv[None, :]
    ang_c = (pos % GRID_W)[:, None] * inv[None, :]
    ang = np.concatenate([ang_r, ang_r, ang_c, ang_c] * (LANES // AT_HD), axis=-1)
    return jnp.asarray(np.cos(ang), dtype=F32), jnp.asarray(np.sin(ang), dtype=F32)


def _layer(x, norm1_g, w_in, lb_f, lb_b, onorm_g, q_norm_g, k_norm_g, w_a, w_b, w_out, norm2_g,
           w_up, conv_w, conv_b, w_down):
    B, L, D = x.shape
    c_hg = 4 * HG_W
    c_gate = 5 * HG_W
    c_at = c_gate + AT_W + 2 * AT_KV_W
    whg = w_in[:, 0:c_hg].astype(BF16)
    wat = w_in[:, c_gate:c_at].astype(BF16)
    wgh = w_in[:, c_hg:c_gate].astype(BF16)
    wgab = w_in[:, c_at:].astype(BF16)
    n1 = norm1_g.reshape(1, D)
    n2 = norm2_g.reshape(1, D)
    qg2 = jnp.tile(q_norm_g.reshape(1, AT_HD), (1, LANES // AT_HD))
    kg2 = jnp.tile(k_norm_g.reshape(1, AT_HD), (1, LANES // AT_HD))
    cos, sin = _rope_tables(L)
    blk = np.arange(AT_W) // AT_HD
    gsum = jnp.asarray((blk[:, None] == blk[None, :]).astype(np.float32) / AT_HD, dtype=BF16)

    hq, hv, gf, gb, aq, ak, vt = _inproj(x, n1, whg, wat, lb_f, lb_b, qg2, kg2, cos, sin, gsum)
    o_f, o_b = _hgrn(hq, hv, gf, gb)
    score_bound = (AT_HD ** 0.5 * LOG2_E * 1.01) * jnp.max(jnp.abs(q_norm_g)) * jnp.max(jnp.abs(k_norm_g))
    flag = (score_bound <= SCORE_LIMIT).astype(jnp.int32).reshape(1)
    o_at = _attention(flag, aq, ak, vt)
    h, hn = _merge(x, o_f, o_b, o_at, n1, wgh, wgab, onorm_g.reshape(1, HG_DK), w_a.astype(BF16),
                   w_b.astype(BF16), w_out.astype(BF16), n2)
    return _ffn(h, hn, w_up.astype(BF16), conv_w, conv_b.reshape(1, 2 * D_FF), w_down.astype(BF16))


def kernel(x, norm1_g, w_in, hg_lb_fwd, hg_lb_bwd, hg_onorm_g, q_norm_g, k_norm_g, w_branch_a,
           w_branch_b, w_out, norm2_g, w_up, conv_w, conv_b, w_down):
    depth = w_in.shape[0]
    assert depth == 1 and hg_lb_fwd.shape[0] == 2
    l = 0
    return _layer(x, norm1_g[l], w_in[l], hg_lb_fwd, hg_lb_bwd, hg_onorm_g[l], q_norm_g[l], k_norm_g[l],
                  w_branch_a[l], w_branch_b[l], w_out[l], norm2_g[l], w_up[l], conv_w[l], conv_b[l],
                  w_down[l])
```

```python
import functools

import jax
import jax.numpy as jnp
import numpy as np
from jax import lax
from jax.experimental import pallas as pl
from jax.experimental.pallas import tpu as pltpu

F32 = jnp.float32
BF16 = jnp.bfloat16

D_MODEL = 1024
GRID_W = 64
HG_HEADS = 4
HG_DK = 128
HG_W = HG_HEADS * HG_DK
HG_CHUNK = 64
AT_HEADS = 8
AT_KV_HEADS = 2
AT_GROUP = AT_HEADS // AT_KV_HEADS
AT_HD = 64
AT_W = AT_HEADS * AT_HD
AT_KV_W = AT_KV_HEADS * AT_HD
ROPE_THETA = 10000.0
ROPE_AXIS_DIM = AT_HD // 2
D_FF = 2816
EPS = 1e-6
LOG2_E = 1.4426950408889634

LANES = 128
BF16_SUBLANES = 16
VMEM_LIMIT = 56 * 1024 * 1024

VT_ROWS = AT_HD + BF16_SUBLANES
HALO = BF16_SUBLANES

TM_PROJ = 512
HG_SUB = 4 * HG_CHUNK
HG_ROWS = 512
TQ = 512
TK = 512
LOOKAHEAD = 2
LOOKAHEAD_BOUNDED = 1
SCORE_LIMIT = 60.0
KV_BLOCKS_PER_ITER = 16
TM_MERGE = 512
TM_FFN = 512
FF_CHUNK = 256
FF_DOWN_GROUPS = 1


def _dot(a, b):
    return jnp.dot(a, b, preferred_element_type=F32)


def _dot_nt(a, b):
    return lax.dot_general(a, b, (((1,), (1,)), ((), ())), preferred_element_type=F32)


def _dot_tn(a, b):
    return lax.dot_general(a, b, (((0,), (0,)), ((), ())), preferred_element_type=F32)


def _sigmoid(x):
    return 1.0 / (1.0 + jnp.exp(-x))


def _rms_rows(x, g):
    ms = jnp.mean(x * x, axis=-1, keepdims=True)
    return x * lax.rsqrt(ms + EPS) * g


def _const_spec(shape):
    nd = len(shape)
    return pl.BlockSpec(shape, lambda *_: (0,) * nd, pipeline_mode=pl.Buffered(1))


def _lower_bound(lb_ref):
    a0 = lb_ref[0:1, :]
    a1 = lb_ref[1:2, :]
    m = jnp.maximum(a0, a1)
    e0 = jnp.exp(a0 - m)
    e1 = jnp.exp(a1 - m)
    return e0 / (e0 + e1)


def _head_rms(t, gsum):
    sq = t * t
    hi = sq.astype(BF16)
    lo = (sq - hi.astype(F32)).astype(BF16)
    ms = _dot(hi, gsum) + _dot(lo, gsum)
    return t * lax.rsqrt(ms + EPS)


def _rope(t, cos, sin_signed, first_half):
    from_hi = pltpu.roll(t, LANES - ROPE_AXIS_DIM // 2, axis=1)
    from_lo = pltpu.roll(t, ROPE_AXIS_DIM // 2, axis=1)
    return t * cos + jnp.where(first_half, from_hi, from_lo) * sin_signed


def _inproj_kernel(x_ref, n1_ref, whg_ref, wat_ref, lbf_ref, lbb_ref, qg_ref, kg_ref,
                   cos_ref, sin_ref, gsum_ref,
                   hq_ref, hv_ref, gf_ref, gb_ref, aq_ref, ak_ref, vt_ref):
    x = x_ref[0]
    tm = x.shape[0]
    u = _rms_rows(x, n1_ref[...]).astype(BF16)

    cos = cos_ref[...]
    sin = sin_ref[...]
    lane = lax.broadcasted_iota(jnp.int32, cos.shape, 1)
    first_half = (lane & (ROPE_AXIS_DIM - 1)) < (ROPE_AXIS_DIM // 2)
    sin_signed = jnp.where(first_half, -sin, sin)
    gsum = gsum_ref[...]

    def silu_q(hq):
        hq_ref[0] = hq * _sigmoid(hq)

    def values(hv):
        hv_ref[0] = hv.astype(BF16)

    def log_forget(out_ref, lb_ref):
        def epilogue(pre):
            lb = _lower_bound(lb_ref)
            out_ref[0] = jnp.log(lb + (1.0 - lb) * _sigmoid(pre))
        return epilogue

    def attn_q(aq):
        qn = _head_rms(aq, gsum)
        scale = AT_HD ** -0.5 * LOG2_E
        for c in range(AT_W // LANES):
            sl = slice(c * LANES, (c + 1) * LANES)
            t = qn[:, sl] * qg_ref[...]
            aq_ref[0, sl, :] = (_rope(t, cos, sin_signed, first_half) * scale).T.astype(BF16)

    def attn_k(ak):
        kn = _head_rms(ak, gsum[0:LANES, 0:LANES]) * kg_ref[...]
        kr = _rope(kn, cos, sin_signed, first_half).astype(BF16)
        for g in range(AT_KV_HEADS):
            ak_ref[0, g] = kr[:, g * AT_HD:(g + 1) * AT_HD]

    def attn_v(av):
        avt = av.T
        row = lax.broadcasted_iota(jnp.int32, (VT_ROWS - AT_HD, tm), 0)
        aug = jnp.where(row == 0, 1.0, 0.0).astype(BF16)
        for g in range(AT_KV_HEADS):
            vt_ref[0, g, 0:AT_HD, :] = avt[g * AT_HD:(g + 1) * AT_HD].astype(BF16)
            vt_ref[0, g, AT_HD:VT_ROWS, :] = aug

    sections = (
        (whg_ref, 0, HG_W, silu_q),
        (whg_ref, HG_W, HG_W, values),
        (whg_ref, 2 * HG_W, HG_W, log_forget(gf_ref, lbf_ref)),
        (whg_ref, 3 * HG_W, HG_W, log_forget(gb_ref, lbb_ref)),
        (wat_ref, 0, AT_W, attn_q),
        (wat_ref, AT_W, AT_KV_W, attn_k),
        (wat_ref, AT_W + AT_KV_W, AT_KV_W, attn_v),
    )

    def project(s):
        w_ref, off, width, _ = sections[s]
        return _dot(u, w_ref[:, off:off + width])

    nxt = project(0)
    for s in range(len(sections)):
        cur = nxt
        if s + 1 < len(sections):
            nxt = project(s + 1)
        sections[s][3](cur)


def _inproj(x, n1, whg, wat, lbf, lbb, qg2, kg2, cos, sin, gsum):
    B, L, D = x.shape
    tm = min(TM_PROJ, L)
    nt = L // tm
    tok = lambda w: pl.BlockSpec((1, tm, w), lambda b, i: (b, i, 0))
    out_shape = (
        jax.ShapeDtypeStruct((B, L, HG_W), F32),
        jax.ShapeDtypeStruct((B, L, HG_W), BF16),
        jax.ShapeDtypeStruct((B, L, HG_W), F32),
        jax.ShapeDtypeStruct((B, L, HG_W), F32),
        jax.ShapeDtypeStruct((B, AT_W, L), BF16),
        jax.ShapeDtypeStruct((B, AT_KV_HEADS, L, AT_HD), BF16),
        jax.ShapeDtypeStruct((B, AT_KV_HEADS, VT_ROWS, L), BF16),
    )
    out_specs = (
        tok(HG_W), tok(HG_W), tok(HG_W), tok(HG_W),
        pl.BlockSpec((1, AT_W, tm), lambda b, i: (b, 0, i)),
        pl.BlockSpec((1, AT_KV_HEADS, tm, AT_HD), lambda b, i: (b, 0, i, 0)),
        pl.BlockSpec((1, AT_KV_HEADS, VT_ROWS, tm), lambda b, i: (b, 0, 0, i)),
    )
    in_specs = [
        tok(D), _const_spec(n1.shape), _const_spec(whg.shape), _const_spec(wat.shape),
        _const_spec(lbf.shape), _const_spec(lbb.shape), _const_spec(qg2.shape), _const_spec(kg2.shape),
        pl.BlockSpec((tm, LANES), lambda b, i: (i, 0)),
        pl.BlockSpec((tm, LANES), lambda b, i: (i, 0)),
        _const_spec(gsum.shape),
    ]
    return pl.pallas_call(
        _inproj_kernel,
        grid=(B, nt),
        in_specs=in_specs,
        out_specs=out_specs,
        out_shape=out_shape,
        compiler_params=pltpu.CompilerParams(
            dimension_semantics=("arbitrary", "arbitrary"), vmem_limit_bytes=VMEM_LIMIT),
        name="inproj",
    )(x, n1, whg, wat, lbf, lbb, qg2, kg2, cos, sin, gsum)


def _chunk_mask(n, reverse):
    shift = HG_CHUNK.bit_length() - 1
    r = lax.broadcasted_iota(jnp.int32, (n, n), 0)
    c = lax.broadcasted_iota(jnp.int32, (n, n), 1)
    same = lax.shift_right_logical(r, shift) == lax.shift_right_logical(c, shift)
    return same & ((c >= r) if reverse else (c <= r))


def _hgrn_kernel(qf_ref, vf_ref, gf_ref, qb_ref, vb_ref, gb_ref, of_ref, ob_ref, stf_ref, stb_ref):
    @pl.when(pl.program_id(1) == 0)
    def _():
        stf_ref[...] = jnp.zeros_like(stf_ref)
        stb_ref[...] = jnp.zeros_like(stb_ref)

    n = HG_SUB
    nsub = qf_ref.shape[1] // n
    nchunk = n // HG_CHUNK
    tri = {rev: _chunk_mask(n, rev) for rev in (False, True)}

    groups = []
    for h in range(HG_HEADS):
        cols = slice(h * HG_DK, (h + 1) * HG_DK)
        for s in range(nsub):
            groups.append(((qf_ref, vf_ref, gf_ref, of_ref), False, h, slice(s * n, (s + 1) * n), cols))
        for s in range(nsub - 1, -1, -1):
            groups.append(((qb_ref, vb_ref, gb_ref, ob_ref), True, h, slice(s * n, (s + 1) * n), cols))

    gs = [refs[2][0, rs, cs] for refs, _, _, rs, cs in groups]
    pos = lax.broadcasted_iota(jnp.int32, (n, HG_DK), 0) & (HG_CHUNK - 1)
    bs = []
    for g, (_, rev, _, _, _) in zip(gs, groups):
        b = g
        step = 1
        while step < HG_CHUNK:
            if rev:
                b = b + jnp.where(pos < HG_CHUNK - step, pltpu.roll(b, n - step, axis=0), 0.0)
            else:
                b = b + jnp.where(pos >= step, pltpu.roll(b, step, axis=0), 0.0)
            step *= 2
        bs.append(b)

    qds, kds, kcs, decs = [], [], [], []
    for g, b, (refs, rev, _, rs, cs) in zip(gs, bs, groups):
        b3 = b.reshape(nchunk, HG_CHUNK, HG_DK)
        tot = b3[:, 0:1, :] if rev else b3[:, HG_CHUNK - 1:HG_CHUNK, :]
        totf = jnp.broadcast_to(tot, b3.shape).reshape(n, HG_DK)
        k = 1.0 - jnp.exp(g)
        qds.append((refs[0][0, rs, cs] * jnp.exp(b)).astype(BF16))
        kds.append((k * jnp.exp(-b)).astype(BF16))
        kcs.append((k * jnp.exp(totf - b)).astype(BF16))
        decs.append(jnp.exp(tot))

    amats = [_dot_nt(qd, kd) for qd, kd in zip(qds, kds)]
    vs = [refs[1][0, rs, cs] for refs, _, _, rs, cs in groups]
    intras = [_dot(jnp.where(tri[rev], a, 0.0).astype(BF16), v)
              for a, v, (_, rev, _, _, _) in zip(amats, vs, groups)]
    chunks = [slice(c * HG_CHUNK, (c + 1) * HG_CHUNK) for c in range(nchunk)]
    uts = [[_dot_tn(v[sl], kc[sl]) for sl in chunks] for v, kc in zip(vs, kcs)]

    states = {}
    for gi, (refs, rev, h, rs, cs) in enumerate(groups):
        st = states[(rev, h)] if (rev, h) in states else (stb_ref if rev else stf_ref)[h]
        outs = [None] * nchunk
        for ci in (range(nchunk - 1, -1, -1) if rev else range(nchunk)):
            outs[ci] = intras[gi][chunks[ci]] + _dot_nt(qds[gi][chunks[ci]], st.astype(BF16))
            st = st * decs[gi][ci] + uts[gi][ci]
        states[(rev, h)] = st
        refs[3][0, rs, cs] = jnp.concatenate(outs, axis=0)
    for (rev, h), st in states.items():
        (stb_ref if rev else stf_ref)[h] = st


def _hgrn(hq, hv, gf, gb):
    B, L, _ = hq.shape
    rows = min(HG_ROWS, L)
    nb = L // rows
    fwd = pl.BlockSpec((1, rows, HG_W), lambda b, i: (b, i, 0))
    bwd = pl.BlockSpec((1, rows, HG_W), lambda b, i: (b, nb - 1 - i, 0))
    o = jax.ShapeDtypeStruct((B, L, HG_W), F32)
    state = pltpu.VMEM((HG_HEADS, HG_DK, HG_DK), F32)
    return pl.pallas_call(
        _hgrn_kernel,
        grid=(B, nb),
        in_specs=[fwd, fwd, fwd, bwd, bwd, bwd],
        out_specs=(fwd, bwd),
        out_shape=(o, o),
        scratch_shapes=[state, state],
        compiler_params=pltpu.CompilerParams(
            dimension_semantics=("arbitrary", "arbitrary"), vmem_limit_bytes=VMEM_LIMIT),
        name="hgrn2",
    )(hq, hv, gf, hq, hv, gb)


def _attn_kernel(flag_ref, q_ref, k_ref, vt_ref, o_ref, acc_ref, *, tk, inner):
    tq = q_ref.shape[2]
    nk = k_ref.shape[2] // tk
    qts = [q_ref[0, j * AT_HD:(j + 1) * AT_HD, :] for j in range(AT_GROUP)]

    def scores(blk, j, stabilised):
        start = pl.multiple_of(blk * tk, tk)
        st = _dot(k_ref[0, 0, pl.ds(start, tk), :], qts[j])
        return st if stabilised else jnp.exp2(st).astype(BF16)

    def run(stabilised, lookahead):
        def body(it, carry):
            pending, state = carry
            pending = list(pending)
            state = list(state)
            for n in range(inner * AT_GROUP):
                blk = it * inner + n // AT_GROUP
                j = n % AT_GROUP
                start = pl.multiple_of(blk * tk, tk)
                vb = vt_ref[0, 0, :, pl.ds(start, tk)]
                st = pending.pop(0)
                ahead = n + lookahead
                pending.append(scores(jnp.minimum(it * inner + ahead // AT_GROUP, nk - 1), ahead % AT_GROUP,
                                      stabilised))
                if stabilised:
                    m, acc = state[j]
                    m_new = jnp.maximum(m, jnp.max(st, axis=0, keepdims=True))
                    p = jnp.exp2(st - m_new).astype(BF16)
                    state[j] = (m_new, jnp.exp2(m - m_new) * acc + _dot(vb, p))
                else:
                    state[j] = state[j] + _dot(vb, st)
            return tuple(pending), tuple(state)

        zero = jnp.zeros((VT_ROWS, tq), F32)
        init = tuple((jnp.full((1, tq), -1e30, F32), zero) if stabilised else zero for _ in range(AT_GROUP))
        first = tuple(scores(min(n // AT_GROUP, nk - 1), n % AT_GROUP, stabilised) for n in range(lookahead))
        _, final = lax.fori_loop(0, nk // inner, body, (first, init))
        for j in range(AT_GROUP):
            acc_ref[j] = final[j][1] if stabilised else final[j]

    @pl.when(flag_ref[0] != 0)
    def _():
        run(False, LOOKAHEAD_BOUNDED)

    @pl.when(flag_ref[0] == 0)
    def _():
        run(True, LOOKAHEAD)

    normed = [acc_ref[j, 0:AT_HD, :] / acc_ref[j, AT_HD:AT_HD + 1, :] for j in range(AT_GROUP)]
    pairs = [jnp.concatenate(normed[2 * p:2 * p + 2], axis=0).T for p in range(AT_GROUP // 2)]
    o_ref[0] = jnp.concatenate(pairs, axis=1).astype(o_ref.dtype)


def _attention(flag, aq, ak, vt):
    B, _, L = aq.shape
    tq = min(TQ, L)
    tk = min(TK, L)
    gw = AT_GROUP * AT_HD
    return pl.pallas_call(
        functools.partial(_attn_kernel, tk=tk, inner=min(KV_BLOCKS_PER_ITER, L // tk)),
        grid=(B, AT_KV_HEADS, L // tq),
        in_specs=[
            pl.BlockSpec(memory_space=pltpu.SMEM),
            pl.BlockSpec((1, gw, tq), lambda b, g, i: (b, g, i)),
            pl.BlockSpec((1, 1, L, AT_HD), lambda b, g, i: (b, g, 0, 0)),
            pl.BlockSpec((1, 1, VT_ROWS, L), lambda b, g, i: (b, g, 0, 0)),
        ],
        out_specs=pl.BlockSpec((1, tq, gw), lambda b, g, i: (b, i, g)),
        out_shape=jax.ShapeDtypeStruct((B, L, AT_W), BF16),
        scratch_shapes=[pltpu.VMEM((AT_GROUP, VT_ROWS, tq), F32)],
        compiler_params=pltpu.CompilerParams(
            dimension_semantics=("arbitrary", "arbitrary", "arbitrary"), vmem_limit_bytes=VMEM_LIMIT),
        name="attention",
    )(flag, aq, ak, vt)


def _merge_kernel(x_ref, of_ref, ob_ref, oat_ref, n1_ref, wgh_ref, wgab_ref, ong_ref, wa_ref, wb_ref,
                  wo_ref, n2_ref, h_ref, hn_ref):
    x = x_ref[0]
    u = _rms_rows(x, n1_ref[...]).astype(BF16)
    hgate = _dot(u, wgh_ref[...])
    yb = _dot(oat_ref[0], wb_ref[...])
    gb = _dot(u, wgab_ref[:, D_MODEL:2 * D_MODEL])
    o = of_ref[0] + ob_ref[0]
    heads = []
    for h in range(HG_HEADS):
        sl = slice(h * HG_DK, (h + 1) * HG_DK)
        heads.append(_rms_rows(o[:, sl], ong_ref[...]))
    oa = (jnp.concatenate(heads, axis=1) * (hgate * _sigmoid(hgate))).astype(BF16)
    ga = _dot(u, wgab_ref[:, 0:D_MODEL])
    gated_b = _sigmoid(gb) * yb
    ya = _dot(oa, wa_ref[...])
    merged = (_sigmoid(ga) * ya + gated_b).astype(BF16)
    h = x + _dot(merged, wo_ref[...])
    h_ref[0] = h
    hn_ref[0] = _rms_rows(h, n2_ref[...]).astype(BF16)


def _merge(x, o_f, o_b, o_at, n1, wgh, wgab, ong, wa, wb, wo, n2):
    B, L, D = x.shape
    tm = min(TM_MERGE, L)
    tok = lambda w: pl.BlockSpec((1, tm, w), lambda b, i: (b, i, 0))
    return pl.pallas_call(
        _merge_kernel,
        grid=(B, L // tm),
        in_specs=[tok(D), tok(HG_W), tok(HG_W), tok(AT_W), _const_spec(n1.shape), _const_spec(wgh.shape),
                  _const_spec(wgab.shape), _const_spec(ong.shape), _const_spec(wa.shape),
                  _const_spec(wb.shape), _const_spec(wo.shape), _const_spec(n2.shape)],
        out_specs=(tok(D), tok(D)),
        out_shape=(jax.ShapeDtypeStruct((B, L, D), F32), jax.ShapeDtypeStruct((B, L, D), BF16)),
        compiler_params=pltpu.CompilerParams(
            dimension_semantics=("arbitrary", "arbitrary"), vmem_limit_bytes=VMEM_LIMIT),
        name="merge",
    )(x, o_f, o_b, o_at, n1, wgh, wgab, ong, wa, wb, wo, n2)


def _ffn_kernel(h_ref, hn_ref, prev_ref, next_ref, wup_ref, cw_ref, cb_ref, wdn_ref, y_ref, ext_ref,
                act_ref):
    i = pl.program_id(1)
    nt = pl.num_programs(1)
    tm = hn_ref.shape[1]
    ext = tm + 2 * HALO
    ext_ref[0:HALO, :] = jnp.where(i > 0, prev_ref[0], jnp.zeros_like(prev_ref[0]))
    ext_ref[HALO:HALO + tm, :] = hn_ref[0]
    ext_ref[HALO + tm:ext, :] = jnp.where(i < nt - 1, next_ref[0], jnp.zeros_like(next_ref[0]))
    hx = ext_ref[...]

    def conv(t, off):
        w = cw_ref[:, off:off + FF_CHUNK]
        before = pltpu.roll(t, 1, axis=0)[HALO:HALO + tm]
        after = pltpu.roll(t, ext - 1, axis=0)[HALO:HALO + tm]
        return (before * w[0:1] + t[HALO:HALO + tm] * w[1:2] + after * w[2:3]
                + cb_ref[:, off:off + FF_CHUNK])

    def up(j):
        off = j * FF_CHUNK
        return (_dot(hx, wup_ref[:, off:off + FF_CHUNK]),
                _dot(hx, wup_ref[:, D_FF + off:D_FF + off + FF_CHUNK]))

    nchunk = D_FF // FF_CHUNK
    bounds = [round(g * nchunk / FF_DOWN_GROUPS) * FF_CHUNK for g in range(FF_DOWN_GROUPS + 1)]
    y = h_ref[0]
    nxt = up(0)
    for j in range(nchunk):
        off = j * FF_CHUNK
        upv, upg = nxt
        if j + 1 < nchunk:
            nxt = up(j + 1)
        val = conv(upv, off)
        gate = conv(upg, D_FF + off)
        act_ref[:, off:off + FF_CHUNK] = (gate * _sigmoid(gate) * val).astype(BF16)
        if off + FF_CHUNK in bounds[1:]:
            lo = bounds[bounds.index(off + FF_CHUNK) - 1]
            y = y + _dot(act_ref[:, lo:off + FF_CHUNK], wdn_ref[lo:off + FF_CHUNK, :])
    y_ref[0] = y


def _ffn(h, hn, wup, cw, cb, wdn):
    B, L, D = h.shape
    tm = min(TM_FFN, L)
    per = tm // HALO
    nh = L // HALO
    tok = pl.BlockSpec((1, tm, D), lambda b, i: (b, i, 0))
    return pl.pallas_call(
        _ffn_kernel,
        grid=(B, L // tm),
        in_specs=[
            tok, tok,
            pl.BlockSpec((1, HALO, D), lambda b, i: (b, jnp.maximum(i * per - 1, 0), 0)),
            pl.BlockSpec((1, HALO, D), lambda b, i: (b, jnp.minimum((i + 1) * per, nh - 1), 0)),
            _const_spec(wup.shape), _const_spec(cw.shape), _const_spec(cb.shape), _const_spec(wdn.shape),
        ],
        out_specs=tok,
        out_shape=jax.ShapeDtypeStruct((B, L, D), F32),
        scratch_shapes=[pltpu.VMEM((tm + 2 * HALO, D), BF16), pltpu.VMEM((tm, D_FF), BF16)],
        compiler_params=pltpu.CompilerParams(
            dimension_semantics=("arbitrary", "arbitrary"), vmem_limit_bytes=VMEM_LIMIT),
        name="convffn",
    )(h, hn, hn, hn, wup, cw, cb, wdn)


def _rope_tables(L):
    pos = np.arange(L)
    inv = ROPE_THETA ** (-np.arange(0, ROPE_AXIS_DIM, 2, dtype=np.float64) / ROPE_AXIS_DIM)
    ang_r = (pos // GRID_W)[:, None] * inv[None, :]
    ang_c = (pos % GRID_W)[:, None] * inv[None, :]
    ang = np.concatenate([ang_r, ang_r, ang_c, ang_c] * (LANES // AT_HD), axis=-1)
    return jnp.asarray(np.cos(ang), dtype=F32), jnp.asarray(np.sin(ang), dtype=F32)


def _layer(x, norm1_g, w_in, lb_f, lb_b, onorm_g, q_norm_g, k_norm_g, w_a, w_b, w_out, norm2_g,
           w_up, conv_w, conv_b, w_down):
    B, L, D = x.shape
    c_hg = 4 * HG_W
    c_gate = 5 * HG_W
    c_at = c_gate + AT_W + 2 * AT_KV_W
    whg = w_in[:, 0:c_hg].astype(BF16)
    wat = w_in[:, c_gate:c_at].astype(BF16)
    wgh = w_in[:, c_hg:c_gate].astype(BF16)
    wgab = w_in[:, c_at:].astype(BF16)
    n1 = norm1_g.reshape(1, D)
    n2 = norm2_g.reshape(1, D)
    qg2 = jnp.tile(q_norm_g.reshape(1, AT_HD), (1, LANES // AT_HD))
    kg2 = jnp.tile(k_norm_g.reshape(1, AT_HD), (1, LANES // AT_HD))
    cos, sin = _rope_tables(L)
    blk = np.arange(AT_W) // AT_HD
    gsum = jnp.asarray((blk[:, None] == blk[None, :]).astype(np.float32) / AT_HD, dtype=BF16)

    hq, hv, gf, gb, aq, ak, vt = _inproj(x, n1, whg, wat, lb_f, lb_b, qg2, kg2, cos, sin, gsum)
    o_f, o_b = _hgrn(hq, hv, gf, gb)
    score_bound = (AT_HD ** 0.5 * LOG2_E * 1.01) * jnp.max(jnp.abs(q_norm_g)) * jnp.max(jnp.abs(k_norm_g))
    flag = (score_bound <= SCORE_LIMIT).astype(jnp.int32).reshape(1)
    o_at = _attention(flag, aq, ak, vt)
    h, hn = _merge(x, o_f, o_b, o_at, n1, wgh, wgab, onorm_g.reshape(1, HG_DK), w_a.astype(BF16),
                   w_b.astype(BF16), w_out.astype(BF16), n2)
    return _ffn(h, hn, w_up.astype(BF16), conv_w, conv_b.reshape(1, 2 * D_FF), w_down.astype(BF16))


def kernel(x, norm1_g, w_in, hg_lb_fwd, hg_lb_bwd, hg_onorm_g, q_norm_g, k_norm_g, w_branch_a,
           w_branch_b, w_out, norm2_g, w_up, conv_w, conv_b, w_down):
    depth = w_in.shape[0]
    assert depth == 1 and hg_lb_fwd.shape[0] == 2
    l = 0
    return _layer(x, norm1_g[l], w_in[l], hg_lb_fwd, hg_lb_bwd, hg_onorm_g[l], q_norm_g[l], k_norm_g[l],
                  w_branch_a[l], w_branch_b[l], w_out[l], norm2_g[l], w_up[l], conv_w[l], conv_b[l],
                  w_down[l])
```

```python
import functools

import jax
import jax.numpy as jnp
import numpy as np
from jax import lax
from jax.experimental import pallas as pl
from jax.experimental.pallas import tpu as pltpu

F32 = jnp.float32
BF16 = jnp.bfloat16

D_MODEL = 1024
GRID_W = 64
HG_HEADS = 4
HG_DK = 128
HG_W = HG_HEADS * HG_DK
HG_CHUNK = 64
AT_HEADS = 8
AT_KV_HEADS = 2
AT_GROUP = AT_HEADS // AT_KV_HEADS
AT_HD = 64
AT_W = AT_HEADS * AT_HD
AT_KV_W = AT_KV_HEADS * AT_HD
ROPE_THETA = 10000.0
ROPE_AXIS_DIM = AT_HD // 2
D_FF = 2816
EPS = 1e-6
LOG2_E = 1.4426950408889634

LANES = 128
BF16_SUBLANES = 16
VMEM_LIMIT = 56 * 1024 * 1024

VT_ROWS = AT_HD + BF16_SUBLANES
HALO = BF16_SUBLANES

TM_PROJ = 512
HG_SUB = 4 * HG_CHUNK
HG_ROWS = 512
TQ = 512
TK = 512
LOOKAHEAD = 2
LOOKAHEAD_BOUNDED = 1
SCORE_LIMIT = 60.0
KV_BLOCKS_PER_ITER = 16
TM_MERGE = 512
TM_FFN = 512
FF_CHUNK = 256
FF_DOWN_GROUPS = 1


def _dot(a, b):
    return jnp.dot(a, b, preferred_element_type=F32)


def _dot_nt(a, b):
    return lax.dot_general(a, b, (((1,), (1,)), ((), ())), preferred_element_type=F32)


def _dot_tn(a, b):
    return lax.dot_general(a, b, (((0,), (0,)), ((), ())), preferred_element_type=F32)


def _sigmoid(x):
    return 1.0 / (1.0 + jnp.exp(-x))


def _rms_rows(x, g):
    ms = jnp.mean(x * x, axis=-1, keepdims=True)
    return x * lax.rsqrt(ms + EPS) * g


def _const_spec(shape):
    nd = len(shape)
    return pl.BlockSpec(shape, lambda *_: (0,) * nd, pipeline_mode=pl.Buffered(1))


def _lower_bound(lb_ref):
    a0 = lb_ref[0:1, :]
    a1 = lb_ref[1:2, :]
    m = jnp.maximum(a0, a1)
    e0 = jnp.exp(a0 - m)
    e1 = jnp.exp(a1 - m)
    return e0 / (e0 + e1)


def _head_rms(t, gsum):
    sq = t * t
    hi = sq.astype(BF16)
    lo = (sq - hi.astype(F32)).astype(BF16)
    ms = _dot(hi, gsum) + _dot(lo, gsum)
    return t * lax.rsqrt(ms + EPS)


def _rope(t, cos, sin_signed, first_half):
    from_hi = pltpu.roll(t, LANES - ROPE_AXIS_DIM // 2, axis=1)
    from_lo = pltpu.roll(t, ROPE_AXIS_DIM // 2, axis=1)
    return t * cos + jnp.where(first_half, from_hi, from_lo) * sin_signed


def _inproj_kernel(x_ref, n1_ref, whg_ref, wat_ref, lbf_ref, lbb_ref, qg_ref, kg_ref,
                   cos_ref, sin_ref, gsum_ref,
                   hq_ref, hv_ref, gf_ref, gb_ref, aq_ref, ak_ref, vt_ref):
    x = x_ref[0]
    tm = x.shape[0]
    u = _rms_rows(x, n1_ref[...]).astype(BF16)

    cos = cos_ref[...]
    sin = sin_ref[...]
    lane = lax.broadcasted_iota(jnp.int32, cos.shape, 1)
    first_half = (lane & (ROPE_AXIS_DIM - 1)) < (ROPE_AXIS_DIM // 2)
    sin_signed = jnp.where(first_half, -sin, sin)
    gsum = gsum_ref[...]

    def silu_q(hq):
        hq_ref[0] = hq * _sigmoid(hq)

    def values(hv):
        hv_ref[0] = hv.astype(BF16)

    def log_forget(out_ref, lb_ref):
        def epilogue(pre):
            lb = _lower_bound(lb_ref)
            out_ref[0] = jnp.log(lb + (1.0 - lb) * _sigmoid(pre))
        return epilogue

    def attn_q(aq):
        qn = _head_rms(aq, gsum)
        scale = AT_HD ** -0.5 * LOG2_E
        for c in range(AT_W // LANES):
            sl = slice(c * LANES, (c + 1) * LANES)
            t = qn[:, sl] * qg_ref[...]
            aq_ref[0, sl, :] = (_rope(t, cos, sin_signed, first_half) * scale).T.astype(BF16)

    def attn_k(ak):
        kn = _head_rms(ak, gsum[0:LANES, 0:LANES]) * kg_ref[...]
        kr = _rope(kn, cos, sin_signed, first_half).astype(BF16)
        for g in range(AT_KV_HEADS):
            ak_ref[0, g] = kr[:, g * AT_HD:(g + 1) * AT_HD]

    def attn_v(av):
        avt = av.T
        row = lax.broadcasted_iota(jnp.int32, (VT_ROWS - AT_HD, tm), 0)
        aug = jnp.where(row == 0, 1.0, 0.0).astype(BF16)
        for g in range(AT_KV_HEADS):
            vt_ref[0, g, 0:AT_HD, :] = avt[g * AT_HD:(g + 1) * AT_HD].astype(BF16)
            vt_ref[0, g, AT_HD:VT_ROWS, :] = aug

    sections = (
        (whg_ref, 0, HG_W, silu_q),
        (whg_ref, HG_W, HG_W, values),
        (whg_ref, 2 * HG_W, HG_W, log_forget(gf_ref, lbf_ref)),
        (whg_ref, 3 * HG_W, HG_W, log_forget(gb_ref, lbb_ref)),
        (wat_ref, 0, AT_W, attn_q),
        (wat_ref, AT_W, AT_KV_W, attn_k),
        (wat_ref, AT_W + AT_KV_W, AT_KV_W, attn_v),
    )

    def project(s):
        w_ref, off, width, _ = sections[s]
        return _dot(u, w_ref[:, off:off + width])

    nxt = project(0)
    for s in range(len(sections)):
        cur = nxt
        if s + 1 < len(sections):
            nxt = project(s + 1)
        sections[s][3](cur)


def _inproj(x, n1, whg, wat, lbf, lbb, qg2, kg2, cos, sin, gsum):
    B, L, D = x.shape
    tm = min(TM_PROJ, L)
    nt = L // tm
    tok = lambda w: pl.BlockSpec((1, tm, w), lambda b, i: (b, i, 0))
    out_shape = (
        jax.ShapeDtypeStruct((B, L, HG_W), F32),
        jax.ShapeDtypeStruct((B, L, HG_W), BF16),
        jax.ShapeDtypeStruct((B, L, HG_W), F32),
        jax.ShapeDtypeStruct((B, L, HG_W), F32),
        jax.ShapeDtypeStruct((B, AT_W, L), BF16),
        jax.ShapeDtypeStruct((B, AT_KV_HEADS, L, AT_HD), BF16),
        jax.ShapeDtypeStruct((B, AT_KV_HEADS, VT_ROWS, L), BF16),
    )
    out_specs = (
        tok(HG_W), tok(HG_W), tok(HG_W), tok(HG_W),
        pl.BlockSpec((1, AT_W, tm), lambda b, i: (b, 0, i)),
        pl.BlockSpec((1, AT_KV_HEADS, tm, AT_HD), lambda b, i: (b, 0, i, 0)),
        pl.BlockSpec((1, AT_KV_HEADS, VT_ROWS, tm), lambda b, i: (b, 0, 0, i)),
    )
    in_specs = [
        tok(D), _const_spec(n1.shape), _const_spec(whg.shape), _const_spec(wat.shape),
        _const_spec(lbf.shape), _const_spec(lbb.shape), _const_spec(qg2.shape), _const_spec(kg2.shape),
        pl.BlockSpec((tm, LANES), lambda b, i: (i, 0)),
        pl.BlockSpec((tm, LANES), lambda b, i: (i, 0)),
        _const_spec(gsum.shape),
    ]
    return pl.pallas_call(
        _inproj_kernel,
        grid=(B, nt),
        in_specs=in_specs,
        out_specs=out_specs,
        out_shape=out_shape,
        compiler_params=pltpu.CompilerParams(
            dimension_semantics=("arbitrary", "arbitrary"), vmem_limit_bytes=VMEM_LIMIT),
        name="inproj",
    )(x, n1, whg, wat, lbf, lbb, qg2, kg2, cos, sin, gsum)


def _chunk_mask(n, reverse):
    shift = HG_CHUNK.bit_length() - 1
    r = lax.broadcasted_iota(jnp.int32, (n, n), 0)
    c = lax.broadcasted_iota(jnp.int32, (n, n), 1)
    same = lax.shift_right_logical(r, shift) == lax.shift_right_logical(c, shift)
    return same & ((c >= r) if reverse else (c <= r))


def _hgrn_kernel(qf_ref, vf_ref, gf_ref, qb_ref, vb_ref, gb_ref, of_ref, ob_ref, stf_ref, stb_ref):
    @pl.when(pl.program_id(1) == 0)
    def _():
        stf_ref[...] = jnp.zeros_like(stf_ref)
        stb_ref[...] = jnp.zeros_like(stb_ref)

    n = HG_SUB
    nsub = qf_ref.shape[1] // n
    nchunk = n // HG_CHUNK
    tri = {rev: _chunk_mask(n, rev) for rev in (False, True)}

    groups = []
    for h in range(HG_HEADS):
        cols = slice(h * HG_DK, (h + 1) * HG_DK)
        for s in range(nsub):
            groups.append(((qf_ref, vf_ref, gf_ref, of_ref), False, h, slice(s * n, (s + 1) * n), cols))
        for s in range(nsub - 1, -1, -1):
            groups.append(((qb_ref, vb_ref, gb_ref, ob_ref), True, h, slice(s * n, (s + 1) * n), cols))

    gs = [refs[2][0, rs, cs] for refs, _, _, rs, cs in groups]
    pos = lax.broadcasted_iota(jnp.int32, (n, HG_DK), 0) & (HG_CHUNK - 1)
    bs = []
    for g, (_, rev, _, _, _) in zip(gs, groups):
        b = g
        step = 1
        while step < HG_CHUNK:
            if rev:
                b = b + jnp.where(pos < HG_CHUNK - step, pltpu.roll(b, n - step, axis=0), 0.0)
            else:
                b = b + jnp.where(pos >= step, pltpu.roll(b, step, axis=0), 0.0)
            step *= 2
        bs.append(b)

    qds, kds, kcs, decs = [], [], [], []
    for g, b, (refs, rev, _, rs, cs) in zip(gs, bs, groups):
        b3 = b.reshape(nchunk, HG_CHUNK, HG_DK)
        tot = b3[:, 0:1, :] if rev else b3[:, HG_CHUNK - 1:HG_CHUNK, :]
        totf = jnp.broadcast_to(tot, b3.shape).reshape(n, HG_DK)
        k = 1.0 - jnp.exp(g)
        qds.append((refs[0][0, rs, cs] * jnp.exp(b)).astype(BF16))
        kds.append((k * jnp.exp(-b)).astype(BF16))
        kcs.append((k * jnp.exp(totf - b)).astype(BF16))
        decs.append(jnp.exp(tot))

    amats = [_dot_nt(qd, kd) for qd, kd in zip(qds, kds)]
    vs = [refs[1][0, rs, cs] for refs, _, _, rs, cs in groups]
    intras = [_dot(jnp.where(tri[rev], a, 0.0).astype(BF16), v)
              for a, v, (_, rev, _, _, _) in zip(amats, vs, groups)]
    chunks = [slice(c * HG_CHUNK, (c + 1) * HG_CHUNK) for c in range(nchunk)]
    uts = [[_dot_tn(v[sl], kc[sl]) for sl in chunks] for v, kc in zip(vs, kcs)]

    chains = {}
    for gi, (_, rev, h, _, _) in enumerate(groups):
        order = range(nchunk - 1, -1, -1) if rev else range(nchunk)
        chains.setdefault((rev, h), []).extend((gi, ci) for ci in order)
    states = {key: (stb_ref if key[0] else stf_ref)[key[1]] for key in chains}
    outs = [[None] * nchunk for _ in groups]
    for t in range(nsub * nchunk):
        for key, steps in chains.items():
            gi, ci = steps[t]
            st = states[key]
            outs[gi][ci] = intras[gi][chunks[ci]] + _dot_nt(qds[gi][chunks[ci]], st.astype(BF16))
            states[key] = st * decs[gi][ci] + uts[gi][ci]
    for gi, (refs, _, _, rs, cs) in enumerate(groups):
        refs[3][0, rs, cs] = jnp.concatenate(outs[gi], axis=0)
    for (rev, h), st in states.items():
        (stb_ref if rev else stf_ref)[h] = st


def _hgrn(hq, hv, gf, gb):
    B, L, _ = hq.shape
    rows = min(HG_ROWS, L)
    nb = L // rows
    fwd = pl.BlockSpec((1, rows, HG_W), lambda b, i: (b, i, 0))
    bwd = pl.BlockSpec((1, rows, HG_W), lambda b, i: (b, nb - 1 - i, 0))
    o = jax.ShapeDtypeStruct((B, L, HG_W), F32)
    state = pltpu.VMEM((HG_HEADS, HG_DK, HG_DK), F32)
    return pl.pallas_call(
        _hgrn_kernel,
        grid=(B, nb),
        in_specs=[fwd, fwd, fwd, bwd, bwd, bwd],
        out_specs=(fwd, bwd),
        out_shape=(o, o),
        scratch_shapes=[state, state],
        compiler_params=pltpu.CompilerParams(
            dimension_semantics=("arbitrary", "arbitrary"), vmem_limit_bytes=VMEM_LIMIT),
        name="hgrn2",
    )(hq, hv, gf, hq, hv, gb)


def _attn_kernel(flag_ref, q_ref, k_ref, vt_ref, o_ref, acc_ref, *, tk, inner):
    tq = q_ref.shape[2]
    nk = k_ref.shape[2] // tk
    qts = [q_ref[0, j * AT_HD:(j + 1) * AT_HD, :] for j in range(AT_GROUP)]

    def scores(blk, j, stabilised):
        start = pl.multiple_of(blk * tk, tk)
        st = _dot(k_ref[0, 0, pl.ds(start, tk), :], qts[j])
        return st if stabilised else jnp.exp2(st).astype(BF16)

    def run(stabilised, lookahead):
        def body(it, carry):
            pending, state = carry
            pending = list(pending)
            state = list(state)
            for n in range(inner * AT_GROUP):
                blk = it * inner + n // AT_GROUP
                j = n % AT_GROUP
                start = pl.multiple_of(blk * tk, tk)
                vb = vt_ref[0, 0, :, pl.ds(start, tk)]
                st = pending.pop(0)
                ahead = n + lookahead
                pending.append(scores(jnp.minimum(it * inner + ahead // AT_GROUP, nk - 1), ahead % AT_GROUP,
                                      stabilised))
                if stabilised:
                    m, acc = state[j]
                    m_new = jnp.maximum(m, jnp.max(st, axis=0, keepdims=True))
                    p = jnp.exp2(st - m_new).astype(BF16)
                    state[j] = (m_new, jnp.exp2(m - m_new) * acc + _dot(vb, p))
                else:
                    state[j] = state[j] + _dot(vb, st)
            return tuple(pending), tuple(state)

        zero = jnp.zeros((VT_ROWS, tq), F32)
        init = tuple((jnp.full((1, tq), -1e30, F32), zero) if stabilised else zero for _ in range(AT_GROUP))
        first = tuple(scores(min(n // AT_GROUP, nk - 1), n % AT_GROUP, stabilised) for n in range(lookahead))
        _, final = lax.fori_loop(0, nk // inner, body, (first, init))
        for j in range(AT_GROUP):
            acc_ref[j] = final[j][1] if stabilised else final[j]

    @pl.when(flag_ref[0] != 0)
    def _():
        run(False, LOOKAHEAD_BOUNDED)

    @pl.when(flag_ref[0] == 0)
    def _():
        run(True, LOOKAHEAD)

    for j in range(AT_GROUP):
        o_ref[0, j * AT_HD:(j + 1) * AT_HD, :] = (
            acc_ref[j, 0:AT_HD, :] / acc_ref[j, AT_HD:AT_HD + 1, :]).astype(o_ref.dtype)


def _attention(flag, aq, ak, vt):
    B, _, L = aq.shape
    tq = min(TQ, L)
    tk = min(TK, L)
    gw = AT_GROUP * AT_HD
    return pl.pallas_call(
        functools.partial(_attn_kernel, tk=tk, inner=min(KV_BLOCKS_PER_ITER, L // tk)),
        grid=(B, AT_KV_HEADS, L // tq),
        in_specs=[
            pl.BlockSpec(memory_space=pltpu.SMEM),
            pl.BlockSpec((1, gw, tq), lambda b, g, i: (b, g, i)),
            pl.BlockSpec((1, 1, L, AT_HD), lambda b, g, i: (b, g, 0, 0)),
            pl.BlockSpec((1, 1, VT_ROWS, L), lambda b, g, i: (b, g, 0, 0)),
        ],
        out_specs=pl.BlockSpec((1, gw, tq), lambda b, g, i: (b, g, i)),
        out_shape=jax.ShapeDtypeStruct((B, AT_W, L), BF16),
        scratch_shapes=[pltpu.VMEM((AT_GROUP, VT_ROWS, tq), F32)],
        compiler_params=pltpu.CompilerParams(
            dimension_semantics=("arbitrary", "arbitrary", "arbitrary"), vmem_limit_bytes=VMEM_LIMIT),
        name="attention",
    )(flag, aq, ak, vt)


def _merge_kernel(x_ref, of_ref, ob_ref, oat_ref, n1_ref, wgh_ref, wgab_ref, ong_ref, wa_ref, wb_ref,
                  wo_ref, n2_ref, h_ref, hn_ref):
    x = x_ref[0]
    u = _rms_rows(x, n1_ref[...]).astype(BF16)
    hgate = _dot(u, wgh_ref[...])
    yb = _dot_tn(oat_ref[0], wb_ref[...])
    gb = _dot(u, wgab_ref[:, D_MODEL:2 * D_MODEL])
    o = of_ref[0] + ob_ref[0]
    heads = []
    for h in range(HG_HEADS):
        sl = slice(h * HG_DK, (h + 1) * HG_DK)
        heads.append(_rms_rows(o[:, sl], ong_ref[...]))
    oa = (jnp.concatenate(heads, axis=1) * (hgate * _sigmoid(hgate))).astype(BF16)
    ga = _dot(u, wgab_ref[:, 0:D_MODEL])
    gated_b = _sigmoid(gb) * yb
    ya = _dot(oa, wa_ref[...])
    merged = (_sigmoid(ga) * ya + gated_b).astype(BF16)
    h = x + _dot(merged, wo_ref[...])
    h_ref[0] = h
    hn_ref[0] = _rms_rows(h, n2_ref[...]).astype(BF16)


def _merge(x, o_f, o_b, o_at, n1, wgh, wgab, ong, wa, wb, wo, n2):
    B, L, D = x.shape
    tm = min(TM_MERGE, L)
    tok = lambda w: pl.BlockSpec((1, tm, w), lambda b, i: (b, i, 0))
    return pl.pallas_call(
        _merge_kernel,
        grid=(B, L // tm),
        in_specs=[tok(D), tok(HG_W), tok(HG_W), pl.BlockSpec((1, AT_W, tm), lambda b, i: (b, 0, i)),
                  _const_spec(n1.shape), _const_spec(wgh.shape),
                  _const_spec(wgab.shape), _const_spec(ong.shape), _const_spec(wa.shape),
                  _const_spec(wb.shape), _const_spec(wo.shape), _const_spec(n2.shape)],
        out_specs=(tok(D), tok(D)),
        out_shape=(jax.ShapeDtypeStruct((B, L, D), F32), jax.ShapeDtypeStruct((B, L, D), BF16)),
        compiler_params=pltpu.CompilerParams(
            dimension_semantics=("arbitrary", "arbitrary"), vmem_limit_bytes=VMEM_LIMIT),
        name="merge",
    )(x, o_f, o_b, o_at, n1, wgh, wgab, ong, wa, wb, wo, n2)


def _ffn_kernel(h_ref, hn_ref, prev_ref, next_ref, wup_ref, cw_ref, cb_ref, wdn_ref, y_ref, ext_ref,
                act_ref):
    i = pl.program_id(1)
    nt = pl.num_programs(1)
    tm = hn_ref.shape[1]
    ext = tm + 2 * HALO
    ext_ref[0:HALO, :] = jnp.where(i > 0, prev_ref[0], jnp.zeros_like(prev_ref[0]))
    ext_ref[HALO:HALO + tm, :] = hn_ref[0]
    ext_ref[HALO + tm:ext, :] = jnp.where(i < nt - 1, next_ref[0], jnp.zeros_like(next_ref[0]))
    hx = ext_ref[...]

    def conv(t, off):
        w = cw_ref[:, off:off + FF_CHUNK]
        before = pltpu.roll(t, 1, axis=0)[HALO:HALO + tm]
        after = pltpu.roll(t, ext - 1, axis=0)[HALO:HALO + tm]
        return (before * w[0:1] + t[HALO:HALO + tm] * w[1:2] + after * w[2:3]
                + cb_ref[:, off:off + FF_CHUNK])

    def up(j):
        off = j * FF_CHUNK
        return (_dot(hx, wup_ref[:, off:off + FF_CHUNK]),
                _dot(hx, wup_ref[:, D_FF + off:D_FF + off + FF_CHUNK]))

    nchunk = D_FF // FF_CHUNK
    bounds = [round(g * nchunk / FF_DOWN_GROUPS) * FF_CHUNK for g in range(FF_DOWN_GROUPS + 1)]
    y = h_ref[0]
    nxt = up(0)
    for j in range(nchunk):
        off = j * FF_CHUNK
        upv, upg = nxt
        if j + 1 < nchunk:
            nxt = up(j + 1)
        val = conv(upv, off)
        gate = conv(upg, D_FF + off)
        act_ref[:, off:off + FF_CHUNK] = (gate * _sigmoid(gate) * val).astype(BF16)
        if off + FF_CHUNK in bounds[1:]:
            lo = bounds[bounds.index(off + FF_CHUNK) - 1]
            y = y + _dot(act_ref[:, lo:off + FF_CHUNK], wdn_ref[lo:off + FF_CHUNK, :])
    y_ref[0] = y


def _ffn(h, hn, wup, cw, cb, wdn):
    B, L, D = h.shape
    tm = min(TM_FFN, L)
    per = tm // HALO
    nh = L // HALO
    tok = pl.BlockSpec((1, tm, D), lambda b, i: (b, i, 0))
    return pl.pallas_call(
        _ffn_kernel,
        grid=(B, L // tm),
        in_specs=[
            tok, tok,
            pl.BlockSpec((1, HALO, D), lambda b, i: (b, jnp.maximum(i * per - 1, 0), 0)),
            pl.BlockSpec((1, HALO, D), lambda b, i: (b, jnp.minimum((i + 1) * per, nh - 1), 0)),
            _const_spec(wup.shape), _const_spec(cw.shape), _const_spec(cb.shape), _const_spec(wdn.shape),
        ],
        out_specs=tok,
        out_shape=jax.ShapeDtypeStruct((B, L, D), F32),
        scratch_shapes=[pltpu.VMEM((tm + 2 * HALO, D), BF16), pltpu.VMEM((tm, D_FF), BF16)],
        compiler_params=pltpu.CompilerParams(
            dimension_semantics=("arbitrary", "arbitrary"), vmem_limit_bytes=VMEM_LIMIT),
        name="convffn",
    )(h, hn, hn, hn, wup, cw, cb, wdn)


def _rope_tables(L):
    pos = np.arange(L)
    inv = ROPE_THETA ** (-np.arange(0, ROPE_AXIS_DIM, 2, dtype=np.float64) / ROPE_AXIS_DIM)
    ang_r = (pos // GRID_W)[:, None] * inv[None, :]
    ang_c = (pos % GRID_W)[:, None] * inv[None, :]
    ang = np.concatenate([ang_r, ang_r, ang_c, ang_c] * (LANES // AT_HD), axis=-1)
    return jnp.asarray(np.cos(ang), dtype=F32), jnp.asarray(np.sin(ang), dtype=F32)


def _layer(x, norm1_g, w_in, lb_f, lb_b, onorm_g, q_norm_g, k_norm_g, w_a, w_b, w_out, norm2_g,
           w_up, conv_w, conv_b, w_down):
    B, L, D = x.shape
    c_hg = 4 * HG_W
    c_gate = 5 * HG_W
    c_at = c_gate + AT_W + 2 * AT_KV_W
    whg = w_in[:, 0:c_hg].astype(BF16)
    wat = w_in[:, c_gate:c_at].astype(BF16)
    wgh = w_in[:, c_hg:c_gate].astype(BF16)
    wgab = w_in[:, c_at:].astype(BF16)
    n1 = norm1_g.reshape(1, D)
    n2 = norm2_g.reshape(1, D)
    qg2 = jnp.tile(q_norm_g.reshape(1, AT_HD), (1, LANES // AT_HD))
    kg2 = jnp.tile(k_norm_g.reshape(1, AT_HD), (1, LANES // AT_HD))
    cos, sin = _rope_tables(L)
    blk = np.arange(AT_W) // AT_HD
    gsum = jnp.asarray((blk[:, None] == blk[None, :]).astype(np.float32) / AT_HD, dtype=BF16)

    hq, hv, gf, gb, aq, ak, vt = _inproj(x, n1, whg, wat, lb_f, lb_b, qg2, kg2, cos, sin, gsum)
    o_f, o_b = _hgrn(hq, hv, gf, gb)
    score_bound = (AT_HD ** 0.5 * LOG2_E * 1.01) * jnp.max(jnp.abs(q_norm_g)) * jnp.max(jnp.abs(k_norm_g))
    flag = (score_bound <= SCORE_LIMIT).astype(jnp.int32).reshape(1)
    o_at = _attention(flag, aq, ak, vt)
    h, hn = _merge(x, o_f, o_b, o_at, n1, wgh, wgab, onorm_g.reshape(1, HG_DK), w_a.astype(BF16),
                   w_b.astype(BF16), w_out.astype(BF16), n2)
    return _ffn(h, hn, w_up.astype(BF16), conv_w, conv_b.reshape(1, 2 * D_FF), w_down.astype(BF16))


def kernel(x, norm1_g, w_in, hg_lb_fwd, hg_lb_bwd, hg_onorm_g, q_norm_g, k_norm_g, w_branch_a,
           w_branch_b, w_out, norm2_g, w_up, conv_w, conv_b, w_down):
    depth = w_in.shape[0]
    assert depth == 1 and hg_lb_fwd.shape[0] == 2
    l = 0
    return _layer(x, norm1_g[l], w_in[l], hg_lb_fwd, hg_lb_bwd, hg_onorm_g[l], q_norm_g[l], k_norm_g[l],
                  w_branch_a[l], w_branch_b[l], w_out[l], norm2_g[l], w_up[l], conv_w[l], conv_b[l],
                  w_down[l])
```

```python
import functools

import jax
import jax.numpy as jnp
import numpy as np
from jax import lax
from jax.experimental import pallas as pl
from jax.experimental.pallas import tpu as pltpu

F32 = jnp.float32
BF16 = jnp.bfloat16
F8 = jnp.float8_e4m3fn

D_MODEL = 1024
GRID_W = 64
HG_HEADS = 4
HG_DK = 128
HG_W = HG_HEADS * HG_DK
HG_CHUNK = 64
AT_HEADS = 8
AT_KV_HEADS = 2
AT_GROUP = AT_HEADS // AT_KV_HEADS
AT_HD = 64
AT_W = AT_HEADS * AT_HD
AT_KV_W = AT_KV_HEADS * AT_HD
ROPE_THETA = 10000.0
ROPE_AXIS_DIM = AT_HD // 2
D_FF = 2816
EPS = 1e-6
LOG2_E = 1.4426950408889634

LANES = 128
BF16_SUBLANES = 16
F8_TARGET_MAX = 224.0
QK_DEPTH = 4 * AT_HD
VMEM_LIMIT = 56 * 1024 * 1024

VT_ROWS = AT_HD + BF16_SUBLANES
HALO = BF16_SUBLANES

TM_PROJ = 512
HG_SUB = 4 * HG_CHUNK
HG_ROWS = 512
TQ = 512
TK = 512
LOOKAHEAD = 2
LOOKAHEAD_BOUNDED = 1
SCORE_LIMIT = 60.0
KV_BLOCKS_PER_ITER = 16
TM_MERGE = 512
TM_FFN = 512
FF_CHUNK = 256
FF_DOWN_GROUPS = 1


def _dot(a, b):
    return jnp.dot(a, b, preferred_element_type=F32)


def _dot_nt(a, b):
    return lax.dot_general(a, b, (((1,), (1,)), ((), ())), preferred_element_type=F32)


def _dot_tn(a, b):
    return lax.dot_general(a, b, (((0,), (0,)), ((), ())), preferred_element_type=F32)


def _sigmoid(x):
    return 1.0 / (1.0 + jnp.exp(-x))


def _rms_rows(x, g):
    ms = jnp.mean(x * x, axis=-1, keepdims=True)
    return x * lax.rsqrt(ms + EPS) * g


def _const_spec(shape):
    nd = len(shape)
    return pl.BlockSpec(shape, lambda *_: (0,) * nd, pipeline_mode=pl.Buffered(1))


def _lower_bound(lb_ref):
    a0 = lb_ref[0:1, :]
    a1 = lb_ref[1:2, :]
    m = jnp.maximum(a0, a1)
    e0 = jnp.exp(a0 - m)
    e1 = jnp.exp(a1 - m)
    return e0 / (e0 + e1)


def _head_rms(t, gsum):
    sq = t * t
    hi = sq.astype(BF16)
    lo = (sq - hi.astype(F32)).astype(BF16)
    ms = _dot(hi, gsum) + _dot(lo, gsum)
    return t * lax.rsqrt(ms + EPS)


def _rope(t, cos, sin_signed, first_half):
    from_hi = pltpu.roll(t, LANES - ROPE_AXIS_DIM // 2, axis=1)
    from_lo = pltpu.roll(t, ROPE_AXIS_DIM // 2, axis=1)
    return t * cos + jnp.where(first_half, from_hi, from_lo) * sin_signed


def _split_f8(t):
    hi = t.astype(F8).astype(F32)
    return hi, t - hi


def _inproj_kernel(x_ref, n1_ref, whg_ref, wat_ref, lbf_ref, lbb_ref, qg_ref, kg_ref, qs_ref, ks_ref,
                   cos_ref, sin_ref, gsum_ref,
                   hq_ref, hv_ref, gf_ref, gb_ref, aq_ref, ak_ref, vt_ref):
    x = x_ref[0]
    tm = x.shape[0]
    u = _rms_rows(x, n1_ref[...]).astype(BF16)

    cos = cos_ref[...]
    sin = sin_ref[...]
    lane = lax.broadcasted_iota(jnp.int32, cos.shape, 1)
    first_half = (lane & (ROPE_AXIS_DIM - 1)) < (ROPE_AXIS_DIM // 2)
    sin_signed = jnp.where(first_half, -sin, sin)
    gsum = gsum_ref[...]

    def silu_q(hq):
        hq_ref[0] = hq * _sigmoid(hq)

    def values(hv):
        hv_ref[0] = hv.astype(BF16)

    def log_forget(out_ref, lb_ref):
        def epilogue(pre):
            lb = _lower_bound(lb_ref)
            out_ref[0] = jnp.log(lb + (1.0 - lb) * _sigmoid(pre))
        return epilogue

    def attn_q(aq):
        qn = _head_rms(aq, gsum)
        for c in range(AT_W // LANES):
            t = qn[:, c * LANES:(c + 1) * LANES] * qg_ref[...]
            hi, lo = _split_f8((_rope(t, cos, sin_signed, first_half) * qs_ref[...]).T)
            for hh in range(LANES // AT_HD):
                rows = slice(hh * AT_HD, (hh + 1) * AT_HD)
                head = c * (LANES // AT_HD) + hh
                aq_ref[0, head * QK_DEPTH:(head + 1) * QK_DEPTH, :] = jnp.concatenate(
                    [hi[rows], lo[rows], hi[rows], lo[rows]], axis=0).astype(F8)

    def attn_k(ak):
        kn = _head_rms(ak, gsum[0:LANES, 0:LANES]) * kg_ref[...]
        hi, lo = _split_f8(_rope(kn, cos, sin_signed, first_half) * ks_ref[...])
        for g in range(AT_KV_HEADS):
            cols = slice(g * AT_HD, (g + 1) * AT_HD)
            ak_ref[0, g] = jnp.concatenate([hi[:, cols], hi[:, cols], lo[:, cols], lo[:, cols]],
                                           axis=1).astype(F8)

    def attn_v(av):
        avt = av.T
        row = lax.broadcasted_iota(jnp.int32, (VT_ROWS - AT_HD, tm), 0)
        aug = jnp.where(row == 0, 1.0, 0.0).astype(BF16)
        for g in range(AT_KV_HEADS):
            vt_ref[0, g, 0:AT_HD, :] = avt[g * AT_HD:(g + 1) * AT_HD].astype(BF16)
            vt_ref[0, g, AT_HD:VT_ROWS, :] = aug

    sections = (
        (whg_ref, 0, HG_W, silu_q),
        (whg_ref, HG_W, HG_W, values),
        (whg_ref, 2 * HG_W, HG_W, log_forget(gf_ref, lbf_ref)),
        (whg_ref, 3 * HG_W, HG_W, log_forget(gb_ref, lbb_ref)),
        (wat_ref, 0, AT_W, attn_q),
        (wat_ref, AT_W, AT_KV_W, attn_k),
        (wat_ref, AT_W + AT_KV_W, AT_KV_W, attn_v),
    )

    def project(s):
        w_ref, off, width, _ = sections[s]
        return _dot(u, w_ref[:, off:off + width])

    nxt = project(0)
    for s in range(len(sections)):
        cur = nxt
        if s + 1 < len(sections):
            nxt = project(s + 1)
        sections[s][3](cur)


def _inproj(x, n1, whg, wat, lbf, lbb, qg2, kg2, qs2, ks2, cos, sin, gsum):
    B, L, D = x.shape
    tm = min(TM_PROJ, L)
    nt = L // tm
    tok = lambda w: pl.BlockSpec((1, tm, w), lambda b, i: (b, i, 0))
    out_shape = (
        jax.ShapeDtypeStruct((B, L, HG_W), F32),
        jax.ShapeDtypeStruct((B, L, HG_W), BF16),
        jax.ShapeDtypeStruct((B, L, HG_W), F32),
        jax.ShapeDtypeStruct((B, L, HG_W), F32),
        jax.ShapeDtypeStruct((B, AT_HEADS * QK_DEPTH, L), F8),
        jax.ShapeDtypeStruct((B, AT_KV_HEADS, L, QK_DEPTH), F8),
        jax.ShapeDtypeStruct((B, AT_KV_HEADS, VT_ROWS, L), BF16),
    )
    out_specs = (
        tok(HG_W), tok(HG_W), tok(HG_W), tok(HG_W),
        pl.BlockSpec((1, AT_HEADS * QK_DEPTH, tm), lambda b, i: (b, 0, i)),
        pl.BlockSpec((1, AT_KV_HEADS, tm, QK_DEPTH), lambda b, i: (b, 0, i, 0)),
        pl.BlockSpec((1, AT_KV_HEADS, VT_ROWS, tm), lambda b, i: (b, 0, 0, i)),
    )
    in_specs = [
        tok(D), _const_spec(n1.shape), _const_spec(whg.shape), _const_spec(wat.shape),
        _const_spec(lbf.shape), _const_spec(lbb.shape), _const_spec(qg2.shape), _const_spec(kg2.shape),
        _const_spec(qs2.shape), _const_spec(ks2.shape),
        pl.BlockSpec((tm, LANES), lambda b, i: (i, 0)),
        pl.BlockSpec((tm, LANES), lambda b, i: (i, 0)),
        _const_spec(gsum.shape),
    ]
    return pl.pallas_call(
        _inproj_kernel,
        grid=(B, nt),
        in_specs=in_specs,
        out_specs=out_specs,
        out_shape=out_shape,
        compiler_params=pltpu.CompilerParams(
            dimension_semantics=("arbitrary", "arbitrary"), vmem_limit_bytes=VMEM_LIMIT),
        name="inproj",
    )(x, n1, whg, wat, lbf, lbb, qg2, kg2, qs2, ks2, cos, sin, gsum)


def _chunk_mask(n, reverse):
    shift = HG_CHUNK.bit_length() - 1
    r = lax.broadcasted_iota(jnp.int32, (n, n), 0)
    c = lax.broadcasted_iota(jnp.int32, (n, n), 1)
    same = lax.shift_right_logical(r, shift) == lax.shift_right_logical(c, shift)
    return same & ((c >= r) if reverse else (c <= r))


def _hgrn_kernel(qf_ref, vf_ref, gf_ref, qb_ref, vb_ref, gb_ref, of_ref, ob_ref, stf_ref, stb_ref):
    @pl.when(pl.program_id(1) == 0)
    def _():
        stf_ref[...] = jnp.zeros_like(stf_ref)
        stb_ref[...] = jnp.zeros_like(stb_ref)

    n = HG_SUB
    nsub = qf_ref.shape[1] // n
    nchunk = n // HG_CHUNK
    tri = {rev: _chunk_mask(n, rev) for rev in (False, True)}

    groups = []
    for h in range(HG_HEADS):
        cols = slice(h * HG_DK, (h + 1) * HG_DK)
        for s in range(nsub):
            groups.append(((qf_ref, vf_ref, gf_ref, of_ref), False, h, slice(s * n, (s + 1) * n), cols))
        for s in range(nsub - 1, -1, -1):
            groups.append(((qb_ref, vb_ref, gb_ref, ob_ref), True, h, slice(s * n, (s + 1) * n), cols))

    gs = [refs[2][0, rs, cs] for refs, _, _, rs, cs in groups]
    pos = lax.broadcasted_iota(jnp.int32, (n, HG_DK), 0) & (HG_CHUNK - 1)
    bs = []
    for g, (_, rev, _, _, _) in zip(gs, groups):
        b = g
        step = 1
        while step < HG_CHUNK:
            if rev:
                b = b + jnp.where(pos < HG_CHUNK - step, pltpu.roll(b, n - step, axis=0), 0.0)
            else:
                b = b + jnp.where(pos >= step, pltpu.roll(b, step, axis=0), 0.0)
            step *= 2
        bs.append(b)

    qds, kds, kcs, decs = [], [], [], []
    for g, b, (refs, rev, _, rs, cs) in zip(gs, bs, groups):
        b3 = b.reshape(nchunk, HG_CHUNK, HG_DK)
        tot = b3[:, 0:1, :] if rev else b3[:, HG_CHUNK - 1:HG_CHUNK, :]
        totf = jnp.broadcast_to(tot, b3.shape).reshape(n, HG_DK)
        k = 1.0 - jnp.exp(g)
        qds.append((refs[0][0, rs, cs] * jnp.exp(b)).astype(BF16))
        kds.append((k * jnp.exp(-b)).astype(BF16))
        kcs.append((k * jnp.exp(totf - b)).astype(BF16))
        decs.append(jnp.exp(tot))

    amats = [_dot_nt(qd, kd) for qd, kd in zip(qds, kds)]
    vs = [refs[1][0, rs, cs] for refs, _, _, rs, cs in groups]
    intras = [_dot(jnp.where(tri[rev], a, 0.0).astype(BF16), v)
              for a, v, (_, rev, _, _, _) in zip(amats, vs, groups)]
    chunks = [slice(c * HG_CHUNK, (c + 1) * HG_CHUNK) for c in range(nchunk)]
    uts = [[_dot_tn(v[sl], kc[sl]) for sl in chunks] for v, kc in zip(vs, kcs)]

    chains = {}
    for gi, (_, rev, h, _, _) in enumerate(groups):
        order = range(nchunk - 1, -1, -1) if rev else range(nchunk)
        chains.setdefault((rev, h), []).extend((gi, ci) for ci in order)
    states = {key: (stb_ref if key[0] else stf_ref)[key[1]] for key in chains}
    outs = [[None] * nchunk for _ in groups]
    for t in range(nsub * nchunk):
        for key, steps in chains.items():
            gi, ci = steps[t]
            st = states[key]
            outs[gi][ci] = intras[gi][chunks[ci]] + _dot_nt(qds[gi][chunks[ci]], st.astype(BF16))
            states[key] = st * decs[gi][ci] + uts[gi][ci]
    for gi, (refs, _, _, rs, cs) in enumerate(groups):
        refs[3][0, rs, cs] = jnp.concatenate(outs[gi], axis=0)
    for (rev, h), st in states.items():
        (stb_ref if rev else stf_ref)[h] = st


def _hgrn(hq, hv, gf, gb):
    B, L, _ = hq.shape
    rows = min(HG_ROWS, L)
    nb = L // rows
    fwd = pl.BlockSpec((1, rows, HG_W), lambda b, i: (b, i, 0))
    bwd = pl.BlockSpec((1, rows, HG_W), lambda b, i: (b, nb - 1 - i, 0))
    o = jax.ShapeDtypeStruct((B, L, HG_W), F32)
    state = pltpu.VMEM((HG_HEADS, HG_DK, HG_DK), F32)
    return pl.pallas_call(
        _hgrn_kernel,
        grid=(B, nb),
        in_specs=[fwd, fwd, fwd, bwd, bwd, bwd],
        out_specs=(fwd, bwd),
        out_shape=(o, o),
        scratch_shapes=[state, state],
        compiler_params=pltpu.CompilerParams(
            dimension_semantics=("arbitrary", "arbitrary"), vmem_limit_bytes=VMEM_LIMIT),
        name="hgrn2",
    )(hq, hv, gf, hq, hv, gb)


def _attn_kernel(flag_ref, inv_ref, q_ref, k_ref, vt_ref, o_ref, acc_ref, *, tk, inner):
    tq = q_ref.shape[2]
    nk = k_ref.shape[2] // tk
    qts = [q_ref[0, j * QK_DEPTH:(j + 1) * QK_DEPTH, :] for j in range(AT_GROUP)]
    inv = inv_ref[0]

    def scores(blk, j, stabilised):
        start = pl.multiple_of(blk * tk, tk)
        st = _dot(k_ref[0, 0, pl.ds(start, tk), :], qts[j]) * inv
        return st if stabilised else jnp.exp2(st).astype(BF16)

    def run(stabilised, lookahead):
        def body(it, carry):
            pending, state = carry
            pending = list(pending)
            state = list(state)
            for n in range(inner * AT_GROUP):
                blk = it * inner + n // AT_GROUP
                j = n % AT_GROUP
                start = pl.multiple_of(blk * tk, tk)
                vb = vt_ref[0, 0, :, pl.ds(start, tk)]
                st = pending.pop(0)
                ahead = n + lookahead
                pending.append(scores(jnp.minimum(it * inner + ahead // AT_GROUP, nk - 1), ahead % AT_GROUP,
                                      stabilised))
                if stabilised:
                    m, acc = state[j]
                    m_new = jnp.maximum(m, jnp.max(st, axis=0, keepdims=True))
                    p = jnp.exp2(st - m_new).astype(BF16)
                    state[j] = (m_new, jnp.exp2(m - m_new) * acc + _dot(vb, p))
                else:
                    state[j] = state[j] + _dot(vb, st)
            return tuple(pending), tuple(state)

        zero = jnp.zeros((VT_ROWS, tq), F32)
        init = tuple((jnp.full((1, tq), -1e30, F32), zero) if stabilised else zero for _ in range(AT_GROUP))
        first = tuple(scores(min(n // AT_GROUP, nk - 1), n % AT_GROUP, stabilised) for n in range(lookahead))
        _, final = lax.fori_loop(0, nk // inner, body, (first, init))
        for j in range(AT_GROUP):
            acc_ref[j] = final[j][1] if stabilised else final[j]

    @pl.when(flag_ref[0] != 0)
    def _():
        run(False, LOOKAHEAD_BOUNDED)

    @pl.when(flag_ref[0] == 0)
    def _():
        run(True, LOOKAHEAD)

    for j in range(AT_GROUP):
        o_ref[0, j * AT_HD:(j + 1) * AT_HD, :] = (
            acc_ref[j, 0:AT_HD, :] / acc_ref[j, AT_HD:AT_HD + 1, :]).astype(o_ref.dtype)


def _attention(flag, inv, aq, ak, vt):
    B, _, L = aq.shape
    tq = min(TQ, L)
    tk = min(TK, L)
    gw = AT_GROUP * AT_HD
    return pl.pallas_call(
        functools.partial(_attn_kernel, tk=tk, inner=min(KV_BLOCKS_PER_ITER, L // tk)),
        grid=(B, AT_KV_HEADS, L // tq),
        in_specs=[
            pl.BlockSpec(memory_space=pltpu.SMEM),
            pl.BlockSpec(memory_space=pltpu.SMEM),
            pl.BlockSpec((1, AT_GROUP * QK_DEPTH, tq), lambda b, g, i: (b, g, i)),
            pl.BlockSpec((1, 1, L, QK_DEPTH), lambda b, g, i: (b, g, 0, 0)),
            pl.BlockSpec((1, 1, VT_ROWS, L), lambda b, g, i: (b, g, 0, 0)),
        ],
        out_specs=pl.BlockSpec((1, gw, tq), lambda b, g, i: (b, g, i)),
        out_shape=jax.ShapeDtypeStruct((B, AT_W, L), BF16),
        scratch_shapes=[pltpu.VMEM((AT_GROUP, VT_ROWS, tq), F32)],
        compiler_params=pltpu.CompilerParams(
            dimension_semantics=("arbitrary", "arbitrary", "arbitrary"), vmem_limit_bytes=VMEM_LIMIT),
        name="attention",
    )(flag, inv, aq, ak, vt)


def _merge_kernel(x_ref, of_ref, ob_ref, oat_ref, n1_ref, wgh_ref, wgab_ref, ong_ref, wa_ref, wb_ref,
                  wo_ref, n2_ref, h_ref, hn_ref):
    x = x_ref[0]
    u = _rms_rows(x, n1_ref[...]).astype(BF16)
    hgate = _dot(u, wgh_ref[...])
    yb = _dot_tn(oat_ref[0], wb_ref[...])
    gb = _dot(u, wgab_ref[:, D_MODEL:2 * D_MODEL])
    o = of_ref[0] + ob_ref[0]
    heads = []
    for h in range(HG_HEADS):
        sl = slice(h * HG_DK, (h + 1) * HG_DK)
        heads.append(_rms_rows(o[:, sl], ong_ref[...]))
    oa = (jnp.concatenate(heads, axis=1) * (hgate * _sigmoid(hgate))).astype(BF16)
    ga = _dot(u, wgab_ref[:, 0:D_MODEL])
    gated_b = _sigmoid(gb) * yb
    ya = _dot(oa, wa_ref[...])
    merged = (_sigmoid(ga) * ya + gated_b).astype(BF16)
    h = x + _dot(merged, wo_ref[...])
    h_ref[0] = h
    hn_ref[0] = _rms_rows(h, n2_ref[...]).astype(BF16)


def _merge(x, o_f, o_b, o_at, n1, wgh, wgab, ong, wa, wb, wo, n2):
    B, L, D = x.shape
    tm = min(TM_MERGE, L)
    tok = lambda w: pl.BlockSpec((1, tm, w), lambda b, i: (b, i, 0))
    return pl.pallas_call(
        _merge_kernel,
        grid=(B, L // tm),
        in_specs=[tok(D), tok(HG_W), tok(HG_W), pl.BlockSpec((1, AT_W, tm), lambda b, i: (b, 0, i)),
                  _const_spec(n1.shape), _const_spec(wgh.shape),
                  _const_spec(wgab.shape), _const_spec(ong.shape), _const_spec(wa.shape),
                  _const_spec(wb.shape), _const_spec(wo.shape), _const_spec(n2.shape)],
        out_specs=(tok(D), tok(D)),
        out_shape=(jax.ShapeDtypeStruct((B, L, D), F32), jax.ShapeDtypeStruct((B, L, D), BF16)),
        compiler_params=pltpu.CompilerParams(
            dimension_semantics=("arbitrary", "arbitrary"), vmem_limit_bytes=VMEM_LIMIT),
        name="merge",
    )(x, o_f, o_b, o_at, n1, wgh, wgab, ong, wa, wb, wo, n2)


def _ffn_kernel(h_ref, hn_ref, prev_ref, next_ref, wup_ref, cw_ref, cb_ref, wdn_ref, y_ref, ext_ref,
                act_ref):
    i = pl.program_id(1)
    nt = pl.num_programs(1)
    tm = hn_ref.shape[1]
    ext = tm + 2 * HALO
    ext_ref[0:HALO, :] = jnp.where(i > 0, prev_ref[0], jnp.zeros_like(prev_ref[0]))
    ext_ref[HALO:HALO + tm, :] = hn_ref[0]
    ext_ref[HALO + tm:ext, :] = jnp.where(i < nt - 1, next_ref[0], jnp.zeros_like(next_ref[0]))
    hx = ext_ref[...]

    def conv(t, off):
        w = cw_ref[:, off:off + FF_CHUNK]
        before = pltpu.roll(t, 1, axis=0)[HALO:HALO + tm]
        after = pltpu.roll(t, ext - 1, axis=0)[HALO:HALO + tm]
        return (before * w[0:1] + t[HALO:HALO + tm] * w[1:2] + after * w[2:3]
                + cb_ref[:, off:off + FF_CHUNK])

    def up(j):
        off = j * FF_CHUNK
        return (_dot(hx, wup_ref[:, off:off + FF_CHUNK]),
                _dot(hx, wup_ref[:, D_FF + off:D_FF + off + FF_CHUNK]))

    nchunk = D_FF // FF_CHUNK
    bounds = [round(g * nchunk / FF_DOWN_GROUPS) * FF_CHUNK for g in range(FF_DOWN_GROUPS + 1)]
    y = h_ref[0]
    nxt = up(0)
    for j in range(nchunk):
        off = j * FF_CHUNK
        upv, upg = nxt
        if j + 1 < nchunk:
            nxt = up(j + 1)
        val = conv(upv, off)
        gate = conv(upg, D_FF + off)
        act_ref[:, off:off + FF_CHUNK] = (gate * _sigmoid(gate) * val).astype(BF16)
        if off + FF_CHUNK in bounds[1:]:
            lo = bounds[bounds.index(off + FF_CHUNK) - 1]
            y = y + _dot(act_ref[:, lo:off + FF_CHUNK], wdn_ref[lo:off + FF_CHUNK, :])
    y_ref[0] = y


def _ffn(h, hn, wup, cw, cb, wdn):
    B, L, D = h.shape
    tm = min(TM_FFN, L)
    per = tm // HALO
    nh = L // HALO
    tok = pl.BlockSpec((1, tm, D), lambda b, i: (b, i, 0))
    return pl.pallas_call(
        _ffn_kernel,
        grid=(B, L // tm),
        in_specs=[
            tok, tok,
            pl.BlockSpec((1, HALO, D), lambda b, i: (b, jnp.maximum(i * per - 1, 0), 0)),
            pl.BlockSpec((1, HALO, D), lambda b, i: (b, jnp.minimum((i + 1) * per, nh - 1), 0)),
            _const_spec(wup.shape), _const_spec(cw.shape), _const_spec(cb.shape), _const_spec(wdn.shape),
        ],
        out_specs=tok,
        out_shape=jax.ShapeDtypeStruct((B, L, D), F32),
        scratch_shapes=[pltpu.VMEM((tm + 2 * HALO, D), BF16), pltpu.VMEM((tm, D_FF), BF16)],
        compiler_params=pltpu.CompilerParams(
            dimension_semantics=("arbitrary", "arbitrary"), vmem_limit_bytes=VMEM_LIMIT),
        name="convffn",
    )(h, hn, hn, hn, wup, cw, cb, wdn)


def _rope_tables(L):
    pos = np.arange(L)
    inv = ROPE_THETA ** (-np.arange(0, ROPE_AXIS_DIM, 2, dtype=np.float64) / ROPE_AXIS_DIM)
    ang_r = (pos // GRID_W)[:, None] * inv[None, :]
    ang_c = (pos % GRID_W)[:, None] * inv[None, :]
    ang = np.concatenate([ang_r, ang_r, ang_c, ang_c] * (LANES // AT_HD), axis=-1)
    return jnp.asarray(np.cos(ang), dtype=F32), jnp.asarray(np.sin(ang), dtype=F32)


def _layer(x, norm1_g, w_in, lb_f, lb_b, onorm_g, q_norm_g, k_norm_g, w_a, w_b, w_out, norm2_g,
           w_up, conv_w, conv_b, w_down):
    B, L, D = x.shape
    c_hg = 4 * HG_W
    c_gate = 5 * HG_W
    c_at = c_gate + AT_W + 2 * AT_KV_W
    whg = w_in[:, 0:c_hg].astype(BF16)
    wat = w_in[:, c_gate:c_at].astype(BF16)
    wgh = w_in[:, c_hg:c_gate].astype(BF16)
    wgab = w_in[:, c_at:].astype(BF16)
    n1 = norm1_g.reshape(1, D)
    n2 = norm2_g.reshape(1, D)
    qg2 = jnp.tile(q_norm_g.reshape(1, AT_HD), (1, LANES // AT_HD))
    kg2 = jnp.tile(k_norm_g.reshape(1, AT_HD), (1, LANES // AT_HD))
    cos, sin = _rope_tables(L)
    blk = np.arange(AT_W) // AT_HD
    gsum = jnp.asarray((blk[:, None] == blk[None, :]).astype(np.float32) / AT_HD, dtype=BF16)

    q_bound = LOG2_E * jnp.max(jnp.abs(q_norm_g))
    k_bound = AT_HD ** 0.5 * jnp.max(jnp.abs(k_norm_g))
    score_bound = 1.01 * q_bound * k_bound
    flag = (score_bound <= SCORE_LIMIT).astype(jnp.int32).reshape(1)

    def f8_shift(bound):
        return jnp.clip(jnp.floor(jnp.log2(F8_TARGET_MAX / jnp.maximum(bound, 1e-30))), -60.0, 60.0)

    eq, ek = f8_shift(q_bound), f8_shift(k_bound)
    qs2 = jnp.full((1, LANES), AT_HD ** -0.5 * LOG2_E, F32) * jnp.exp2(eq)
    ks2 = jnp.full((1, LANES), 1.0, F32) * jnp.exp2(ek)
    inv = jnp.exp2(-(eq + ek)).astype(F32).reshape(1)

    hq, hv, gf, gb, aq, ak, vt = _inproj(x, n1, whg, wat, lb_f, lb_b, qg2, kg2, qs2, ks2, cos, sin, gsum)
    o_f, o_b = _hgrn(hq, hv, gf, gb)
    o_at = _attention(flag, inv, aq, ak, vt)
    h, hn = _merge(x, o_f, o_b, o_at, n1, wgh, wgab, onorm_g.reshape(1, HG_DK), w_a.astype(BF16),
                   w_b.astype(BF16), w_out.astype(BF16), n2)
    return _ffn(h, hn, w_up.astype(BF16), conv_w, conv_b.reshape(1, 2 * D_FF), w_down.astype(BF16))


def kernel(x, norm1_g, w_in, hg_lb_fwd, hg_lb_bwd, hg_onorm_g, q_norm_g, k_norm_g, w_branch_a,
           w_branch_b, w_out, norm2_g, w_up, conv_w, conv_b, w_down):
    depth = w_in.shape[0]
    assert depth == 1 and hg_lb_fwd.shape[0] == 2
    l = 0
    return _layer(x, norm1_g[l], w_in[l], hg_lb_fwd, hg_lb_bwd, hg_onorm_g[l], q_norm_g[l], k_norm_g[l],
                  w_branch_a[l], w_branch_b[l], w_out[l], norm2_g[l], w_up[l], conv_w[l], conv_b[l],
                  w_down[l])
```

```python
import functools

import jax
import jax.numpy as jnp
import numpy as np
from jax import lax
from jax.experimental import pallas as pl
from jax.experimental.pallas import tpu as pltpu

F32 = jnp.float32
BF16 = jnp.bfloat16
F8 = jnp.float8_e4m3fn

D_MODEL = 1024
GRID_W = 64
HG_HEADS = 4
HG_DK = 128
HG_W = HG_HEADS * HG_DK
HG_CHUNK = 64
AT_HEADS = 8
AT_KV_HEADS = 2
AT_GROUP = AT_HEADS // AT_KV_HEADS
AT_HD = 64
AT_W = AT_HEADS * AT_HD
AT_KV_W = AT_KV_HEADS * AT_HD
ROPE_THETA = 10000.0
ROPE_AXIS_DIM = AT_HD // 2
D_FF = 2816
EPS = 1e-6
LOG2_E = 1.4426950408889634

LANES = 128
BF16_SUBLANES = 16
F8_TARGET_MAX = 224.0
QK_DEPTH = 4 * AT_HD
VMEM_LIMIT = 56 * 1024 * 1024

VT_ROWS = AT_HD + BF16_SUBLANES
HALO = BF16_SUBLANES

TM_PROJ = 512
HG_SUB = 4 * HG_CHUNK
HG_ROWS = 512
TQ = 512
TK = 512
LOOKAHEAD = 2
LOOKAHEAD_BOUNDED = 1
SCORE_LIMIT = 60.0
KV_BLOCKS_PER_ITER = 8
TM_MERGE = 512
TM_FFN = 512
FF_CHUNK = 256
FF_DOWN_GROUPS = 1


def _dot(a, b):
    return jnp.dot(a, b, preferred_element_type=F32)


def _dot_nt(a, b):
    return lax.dot_general(a, b, (((1,), (1,)), ((), ())), preferred_element_type=F32)


def _dot_tn(a, b):
    return lax.dot_general(a, b, (((0,), (0,)), ((), ())), preferred_element_type=F32)


def _sigmoid(x):
    return 1.0 / (1.0 + jnp.exp(-x))


def _rms_rows(x, g):
    ms = jnp.mean(x * x, axis=-1, keepdims=True)
    return x * lax.rsqrt(ms + EPS) * g


def _const_spec(shape):
    nd = len(shape)
    return pl.BlockSpec(shape, lambda *_: (0,) * nd, pipeline_mode=pl.Buffered(1))


def _lower_bound(lb_ref):
    a0 = lb_ref[0:1, :]
    a1 = lb_ref[1:2, :]
    m = jnp.maximum(a0, a1)
    e0 = jnp.exp(a0 - m)
    e1 = jnp.exp(a1 - m)
    return e0 / (e0 + e1)


def _head_rms(t, gsum):
    sq = t * t
    hi = sq.astype(BF16)
    lo = (sq - hi.astype(F32)).astype(BF16)
    ms = _dot(hi, gsum) + _dot(lo, gsum)
    return t * lax.rsqrt(ms + EPS)


def _rope(t, cos, sin_signed, first_half):
    from_hi = pltpu.roll(t, LANES - ROPE_AXIS_DIM // 2, axis=1)
    from_lo = pltpu.roll(t, ROPE_AXIS_DIM // 2, axis=1)
    return t * cos + jnp.where(first_half, from_hi, from_lo) * sin_signed


def _split_f8(t):
    hi = t.astype(F8).astype(F32)
    return hi, t - hi


def _inproj_kernel(x_ref, n1_ref, whg_ref, wat_ref, lbf_ref, lbb_ref, qg_ref, kg_ref, qs_ref, ks_ref,
                   cos_ref, sin_ref, gsum_ref,
                   hq_ref, hv_ref, gf_ref, gb_ref, aq_ref, ak_ref, vt_ref):
    x = x_ref[0]
    tm = x.shape[0]
    u = _rms_rows(x, n1_ref[...]).astype(BF16)

    cos = cos_ref[...]
    sin = sin_ref[...]
    lane = lax.broadcasted_iota(jnp.int32, cos.shape, 1)
    first_half = (lane & (ROPE_AXIS_DIM - 1)) < (ROPE_AXIS_DIM // 2)
    sin_signed = jnp.where(first_half, -sin, sin)
    gsum = gsum_ref[...]

    def silu_q(hq):
        hq_ref[0] = hq * _sigmoid(hq)

    def values(hv):
        hv_ref[0] = hv.astype(BF16)

    def log_forget(out_ref, lb_ref):
        def epilogue(pre):
            lb = _lower_bound(lb_ref)
            out_ref[0] = jnp.log(lb + (1.0 - lb) * _sigmoid(pre))
        return epilogue

    def attn_q(aq):
        qn = _head_rms(aq, gsum)
        for c in range(AT_W // LANES):
            t = qn[:, c * LANES:(c + 1) * LANES] * qg_ref[...]
            hi, lo = _split_f8((_rope(t, cos, sin_signed, first_half) * qs_ref[...]).T)
            for hh in range(LANES // AT_HD):
                rows = slice(hh * AT_HD, (hh + 1) * AT_HD)
                head = c * (LANES // AT_HD) + hh
                aq_ref[0, head * QK_DEPTH:(head + 1) * QK_DEPTH, :] = jnp.concatenate(
                    [hi[rows], lo[rows], hi[rows], lo[rows]], axis=0).astype(F8)

    def attn_k(ak):
        kn = _head_rms(ak, gsum[0:LANES, 0:LANES]) * kg_ref[...]
        hi, lo = _split_f8(_rope(kn, cos, sin_signed, first_half) * ks_ref[...])
        for g in range(AT_KV_HEADS):
            cols = slice(g * AT_HD, (g + 1) * AT_HD)
            ak_ref[0, g] = jnp.concatenate([hi[:, cols], hi[:, cols], lo[:, cols], lo[:, cols]],
                                           axis=1).astype(F8)

    def attn_v(av):
        avt = av.T
        row = lax.broadcasted_iota(jnp.int32, (VT_ROWS - AT_HD, tm), 0)
        aug = jnp.where(row == 0, 1.0, 0.0).astype(BF16)
        for g in range(AT_KV_HEADS):
            vt_ref[0, g, 0:AT_HD, :] = avt[g * AT_HD:(g + 1) * AT_HD].astype(BF16)
            vt_ref[0, g, AT_HD:VT_ROWS, :] = aug

    sections = (
        (whg_ref, 0, HG_W, silu_q),
        (whg_ref, HG_W, HG_W, values),
        (whg_ref, 2 * HG_W, HG_W, log_forget(gf_ref, lbf_ref)),
        (whg_ref, 3 * HG_W, HG_W, log_forget(gb_ref, lbb_ref)),
        (wat_ref, 0, AT_W, attn_q),
        (wat_ref, AT_W, AT_KV_W, attn_k),
        (wat_ref, AT_W + AT_KV_W, AT_KV_W, attn_v),
    )

    def project(s):
        w_ref, off, width, _ = sections[s]
        return _dot(u, w_ref[:, off:off + width])

    nxt = project(0)
    for s in range(len(sections)):
        cur = nxt
        if s + 1 < len(sections):
            nxt = project(s + 1)
        sections[s][3](cur)


def _inproj(x, n1, whg, wat, lbf, lbb, qg2, kg2, qs2, ks2, cos, sin, gsum):
    B, L, D = x.shape
    tm = min(TM_PROJ, L)
    nt = L // tm
    tok = lambda w: pl.BlockSpec((1, tm, w), lambda b, i: (b, i, 0))
    out_shape = (
        jax.ShapeDtypeStruct((B, L, HG_W), F32),
        jax.ShapeDtypeStruct((B, L, HG_W), BF16),
        jax.ShapeDtypeStruct((B, L, HG_W), F32),
        jax.ShapeDtypeStruct((B, L, HG_W), F32),
        jax.ShapeDtypeStruct((B, AT_HEADS * QK_DEPTH, L), F8),
        jax.ShapeDtypeStruct((B, AT_KV_HEADS, L, QK_DEPTH), F8),
        jax.ShapeDtypeStruct((B, AT_KV_HEADS, VT_ROWS, L), BF16),
    )
    out_specs = (
        tok(HG_W), tok(HG_W), tok(HG_W), tok(HG_W),
        pl.BlockSpec((1, AT_HEADS * QK_DEPTH, tm), lambda b, i: (b, 0, i)),
        pl.BlockSpec((1, AT_KV_HEADS, tm, QK_DEPTH), lambda b, i: (b, 0, i, 0)),
        pl.BlockSpec((1, AT_KV_HEADS, VT_ROWS, tm), lambda b, i: (b, 0, 0, i)),
    )
    in_specs = [
        tok(D), _const_spec(n1.shape), _const_spec(whg.shape), _const_spec(wat.shape),
        _const_spec(lbf.shape), _const_spec(lbb.shape), _const_spec(qg2.shape), _const_spec(kg2.shape),
        _const_spec(qs2.shape), _const_spec(ks2.shape),
        pl.BlockSpec((tm, LANES), lambda b, i: (i, 0)),
        pl.BlockSpec((tm, LANES), lambda b, i: (i, 0)),
        _const_spec(gsum.shape),
    ]
    return pl.pallas_call(
        _inproj_kernel,
        grid=(B, nt),
        in_specs=in_specs,
        out_specs=out_specs,
        out_shape=out_shape,
        compiler_params=pltpu.CompilerParams(
            dimension_semantics=("arbitrary", "arbitrary"), vmem_limit_bytes=VMEM_LIMIT),
        name="inproj",
    )(x, n1, whg, wat, lbf, lbb, qg2, kg2, qs2, ks2, cos, sin, gsum)


def _chunk_mask(n, reverse):
    shift = HG_CHUNK.bit_length() - 1
    r = lax.broadcasted_iota(jnp.int32, (n, n), 0)
    c = lax.broadcasted_iota(jnp.int32, (n, n), 1)
    same = lax.shift_right_logical(r, shift) == lax.shift_right_logical(c, shift)
    return same & ((c >= r) if reverse else (c <= r))


def _hgrn_kernel(qf_ref, vf_ref, gf_ref, qb_ref, vb_ref, gb_ref, of_ref, ob_ref, stf_ref, stb_ref):
    @pl.when(pl.program_id(1) == 0)
    def _():
        stf_ref[...] = jnp.zeros_like(stf_ref)
        stb_ref[...] = jnp.zeros_like(stb_ref)

    n = HG_SUB
    nsub = qf_ref.shape[1] // n
    nchunk = n // HG_CHUNK
    tri = {rev: _chunk_mask(n, rev) for rev in (False, True)}

    groups = []
    for h in range(HG_HEADS):
        cols = slice(h * HG_DK, (h + 1) * HG_DK)
        for s in range(nsub):
            groups.append(((qf_ref, vf_ref, gf_ref, of_ref), False, h, slice(s * n, (s + 1) * n), cols))
        for s in range(nsub - 1, -1, -1):
            groups.append(((qb_ref, vb_ref, gb_ref, ob_ref), True, h, slice(s * n, (s + 1) * n), cols))

    gs = [refs[2][0, rs, cs] for refs, _, _, rs, cs in groups]
    pos = lax.broadcasted_iota(jnp.int32, (n, HG_DK), 0) & (HG_CHUNK - 1)
    bs = []
    for g, (_, rev, _, _, _) in zip(gs, groups):
        b = g
        step = 1
        while step < HG_CHUNK:
            if rev:
                b = b + jnp.where(pos < HG_CHUNK - step, pltpu.roll(b, n - step, axis=0), 0.0)
            else:
                b = b + jnp.where(pos >= step, pltpu.roll(b, step, axis=0), 0.0)
            step *= 2
        bs.append(b)

    qds, kds, kcs, decs = [], [], [], []
    for g, b, (refs, rev, _, rs, cs) in zip(gs, bs, groups):
        b3 = b.reshape(nchunk, HG_CHUNK, HG_DK)
        tot = b3[:, 0:1, :] if rev else b3[:, HG_CHUNK - 1:HG_CHUNK, :]
        totf = jnp.broadcast_to(tot, b3.shape).reshape(n, HG_DK)
        k = 1.0 - jnp.exp(g)
        qds.append((refs[0][0, rs, cs] * jnp.exp(b)).astype(BF16))
        kds.append((k * jnp.exp(-b)).astype(BF16))
        kcs.append((k * jnp.exp(totf - b)).astype(BF16))
        decs.append(jnp.exp(tot))

    amats = [_dot_nt(qd, kd) for qd, kd in zip(qds, kds)]
    vs = [refs[1][0, rs, cs] for refs, _, _, rs, cs in groups]
    intras = [_dot(jnp.where(tri[rev], a, 0.0).astype(BF16), v)
              for a, v, (_, rev, _, _, _) in zip(amats, vs, groups)]
    chunks = [slice(c * HG_CHUNK, (c + 1) * HG_CHUNK) for c in range(nchunk)]
    uts = [[_dot_tn(v[sl], kc[sl]) for sl in chunks] for v, kc in zip(vs, kcs)]

    chains = {}
    for gi, (_, rev, h, _, _) in enumerate(groups):
        order = range(nchunk - 1, -1, -1) if rev else range(nchunk)
        chains.setdefault((rev, h), []).extend((gi, ci) for ci in order)
    states = {key: (stb_ref if key[0] else stf_ref)[key[1]] for key in chains}
    outs = [[None] * nchunk for _ in groups]
    for t in range(nsub * nchunk):
        for key, steps in chains.items():
            gi, ci = steps[t]
            st = states[key]
            outs[gi][ci] = intras[gi][chunks[ci]] + _dot_nt(qds[gi][chunks[ci]], st.astype(BF16))
            states[key] = st * decs[gi][ci] + uts[gi][ci]
    for gi, (refs, _, _, rs, cs) in enumerate(groups):
        refs[3][0, rs, cs] = jnp.concatenate(outs[gi], axis=0)
    for (rev, h), st in states.items():
        (stb_ref if rev else stf_ref)[h] = st


def _hgrn(hq, hv, gf, gb):
    B, L, _ = hq.shape
    rows = min(HG_ROWS, L)
    nb = L // rows
    fwd = pl.BlockSpec((1, rows, HG_W), lambda b, i: (b, i, 0))
    bwd = pl.BlockSpec((1, rows, HG_W), lambda b, i: (b, nb - 1 - i, 0))
    o = jax.ShapeDtypeStruct((B, L, HG_W), F32)
    state = pltpu.VMEM((HG_HEADS, HG_DK, HG_DK), F32)
    return pl.pallas_call(
        _hgrn_kernel,
        grid=(B, nb),
        in_specs=[fwd, fwd, fwd, bwd, bwd, bwd],
        out_specs=(fwd, bwd),
        out_shape=(o, o),
        scratch_shapes=[state, state],
        compiler_params=pltpu.CompilerParams(
            dimension_semantics=("arbitrary", "arbitrary"), vmem_limit_bytes=VMEM_LIMIT),
        name="hgrn2",
    )(hq, hv, gf, hq, hv, gb)


def _attn_kernel(flag_ref, inv_ref, q_ref, k_ref, vt_ref, o_ref, acc_ref, *, tk, inner):
    tq = q_ref.shape[2]
    nk = k_ref.shape[2] // tk
    qts = [q_ref[0, j * QK_DEPTH:(j + 1) * QK_DEPTH, :] for j in range(AT_GROUP)]
    inv = inv_ref[0]

    def scores(blk, j, stabilised):
        start = pl.multiple_of(blk * tk, tk)
        st = _dot(k_ref[0, 0, pl.ds(start, tk), :], qts[j]) * inv
        return st if stabilised else jnp.exp2(st).astype(BF16)

    def run(stabilised, lookahead):
        def body(it, carry):
            pending, state = carry
            pending = list(pending)
            state = list(state)
            for n in range(inner * AT_GROUP):
                blk = it * inner + n // AT_GROUP
                j = n % AT_GROUP
                start = pl.multiple_of(blk * tk, tk)
                vb = vt_ref[0, 0, :, pl.ds(start, tk)]
                st = pending.pop(0)
                ahead = n + lookahead
                pending.append(scores(jnp.minimum(it * inner + ahead // AT_GROUP, nk - 1), ahead % AT_GROUP,
                                      stabilised))
                if stabilised:
                    m, acc = state[j]
                    m_new = jnp.maximum(m, jnp.max(st, axis=0, keepdims=True))
                    p = jnp.exp2(st - m_new).astype(BF16)
                    state[j] = (m_new, jnp.exp2(m - m_new) * acc + _dot(vb, p))
                else:
                    state[j] = state[j] + _dot(vb, st)
            return tuple(pending), tuple(state)

        zero = jnp.zeros((VT_ROWS, tq), F32)
        init = tuple((jnp.full((1, tq), -1e30, F32), zero) if stabilised else zero for _ in range(AT_GROUP))
        first = tuple(scores(min(n // AT_GROUP, nk - 1), n % AT_GROUP, stabilised) for n in range(lookahead))
        _, final = lax.fori_loop(0, nk // inner, body, (first, init))
        for j in range(AT_GROUP):
            acc_ref[j] = final[j][1] if stabilised else final[j]

    @pl.when(flag_ref[0] != 0)
    def _():
        run(False, LOOKAHEAD_BOUNDED)

    @pl.when(flag_ref[0] == 0)
    def _():
        run(True, LOOKAHEAD)

    for j in range(AT_GROUP):
        o_ref[0, j * AT_HD:(j + 1) * AT_HD, :] = (
            acc_ref[j, 0:AT_HD, :] / acc_ref[j, AT_HD:AT_HD + 1, :]).astype(o_ref.dtype)


def _attention(flag, inv, aq, ak, vt):
    B, _, L = aq.shape
    tq = min(TQ, L)
    tk = min(TK, L)
    gw = AT_GROUP * AT_HD
    return pl.pallas_call(
        functools.partial(_attn_kernel, tk=tk, inner=min(KV_BLOCKS_PER_ITER, L // tk)),
        grid=(B, AT_KV_HEADS, L // tq),
        in_specs=[
            pl.BlockSpec(memory_space=pltpu.SMEM),
            pl.BlockSpec(memory_space=pltpu.SMEM),
            pl.BlockSpec((1, AT_GROUP * QK_DEPTH, tq), lambda b, g, i: (b, g, i)),
            pl.BlockSpec((1, 1, L, QK_DEPTH), lambda b, g, i: (b, g, 0, 0)),
            pl.BlockSpec((1, 1, VT_ROWS, L), lambda b, g, i: (b, g, 0, 0)),
        ],
        out_specs=pl.BlockSpec((1, gw, tq), lambda b, g, i: (b, g, i)),
        out_shape=jax.ShapeDtypeStruct((B, AT_W, L), BF16),
        scratch_shapes=[pltpu.VMEM((AT_GROUP, VT_ROWS, tq), F32)],
        compiler_params=pltpu.CompilerParams(
            dimension_semantics=("arbitrary", "arbitrary", "arbitrary"), vmem_limit_bytes=VMEM_LIMIT),
        name="attention",
    )(flag, inv, aq, ak, vt)


def _merge_kernel(x_ref, of_ref, ob_ref, oat_ref, n1_ref, wgh_ref, wgab_ref, ong_ref, wa_ref, wb_ref,
                  wo_ref, n2_ref, h_ref, hn_ref):
    x = x_ref[0]
    u = _rms_rows(x, n1_ref[...]).astype(BF16)
    hgate = _dot(u, wgh_ref[...])
    yb = _dot_tn(oat_ref[0], wb_ref[...])
    gb = _dot(u, wgab_ref[:, D_MODEL:2 * D_MODEL])
    o = of_ref[0] + ob_ref[0]
    heads = []
    for h in range(HG_HEADS):
        sl = slice(h * HG_DK, (h + 1) * HG_DK)
        heads.append(_rms_rows(o[:, sl], ong_ref[...]))
    oa = (jnp.concatenate(heads, axis=1) * (hgate * _sigmoid(hgate))).astype(BF16)
    ga = _dot(u, wgab_ref[:, 0:D_MODEL])
    gated_b = _sigmoid(gb) * yb
    ya = _dot(oa, wa_ref[...])
    merged = (_sigmoid(ga) * ya + gated_b).astype(BF16)
    h = x + _dot(merged, wo_ref[...])
    h_ref[0] = h
    hn_ref[0] = _rms_rows(h, n2_ref[...]).astype(BF16)


def _merge(x, o_f, o_b, o_at, n1, wgh, wgab, ong, wa, wb, wo, n2):
    B, L, D = x.shape
    tm = min(TM_MERGE, L)
    tok = lambda w: pl.BlockSpec((1, tm, w), lambda b, i: (b, i, 0))
    return pl.pallas_call(
        _merge_kernel,
        grid=(B, L // tm),
        in_specs=[tok(D), tok(HG_W), tok(HG_W), pl.BlockSpec((1, AT_W, tm), lambda b, i: (b, 0, i)),
                  _const_spec(n1.shape), _const_spec(wgh.shape),
                  _const_spec(wgab.shape), _const_spec(ong.shape), _const_spec(wa.shape),
                  _const_spec(wb.shape), _const_spec(wo.shape), _const_spec(n2.shape)],
        out_specs=(tok(D), tok(D)),
        out_shape=(jax.ShapeDtypeStruct((B, L, D), F32), jax.ShapeDtypeStruct((B, L, D), BF16)),
        compiler_params=pltpu.CompilerParams(
            dimension_semantics=("arbitrary", "arbitrary"), vmem_limit_bytes=VMEM_LIMIT),
        name="merge",
    )(x, o_f, o_b, o_at, n1, wgh, wgab, ong, wa, wb, wo, n2)


def _ffn_kernel(h_ref, hn_ref, prev_ref, next_ref, wup_ref, cw_ref, cb_ref, wdn_ref, y_ref, ext_ref,
                act_ref):
    i = pl.program_id(1)
    nt = pl.num_programs(1)
    tm = hn_ref.shape[1]
    ext = tm + 2 * HALO
    ext_ref[0:HALO, :] = jnp.where(i > 0, prev_ref[0], jnp.zeros_like(prev_ref[0]))
    ext_ref[HALO:HALO + tm, :] = hn_ref[0]
    ext_ref[HALO + tm:ext, :] = jnp.where(i < nt - 1, next_ref[0], jnp.zeros_like(next_ref[0]))
    hx = ext_ref[...]

    def conv(t, off):
        w = cw_ref[:, off:off + FF_CHUNK]
        before = pltpu.roll(t, 1, axis=0)[HALO:HALO + tm]
        after = pltpu.roll(t, ext - 1, axis=0)[HALO:HALO + tm]
        return (before * w[0:1] + t[HALO:HALO + tm] * w[1:2] + after * w[2:3]
                + cb_ref[:, off:off + FF_CHUNK])

    def up(j):
        off = j * FF_CHUNK
        return (_dot(hx, wup_ref[:, off:off + FF_CHUNK]),
                _dot(hx, wup_ref[:, D_FF + off:D_FF + off + FF_CHUNK]))

    nchunk = D_FF // FF_CHUNK
    bounds = [round(g * nchunk / FF_DOWN_GROUPS) * FF_CHUNK for g in range(FF_DOWN_GROUPS + 1)]
    y = h_ref[0]
    nxt = up(0)
    for j in range(nchunk):
        off = j * FF_CHUNK
        upv, upg = nxt
        if j + 1 < nchunk:
            nxt = up(j + 1)
        val = conv(upv, off)
        gate = conv(upg, D_FF + off)
        act_ref[:, off:off + FF_CHUNK] = (gate * _sigmoid(gate) * val).astype(BF16)
        if off + FF_CHUNK in bounds[1:]:
            lo = bounds[bounds.index(off + FF_CHUNK) - 1]
            y = y + _dot(act_ref[:, lo:off + FF_CHUNK], wdn_ref[lo:off + FF_CHUNK, :])
    y_ref[0] = y


def _ffn(h, hn, wup, cw, cb, wdn):
    B, L, D = h.shape
    tm = min(TM_FFN, L)
    per = tm // HALO
    nh = L // HALO
    tok = pl.BlockSpec((1, tm, D), lambda b, i: (b, i, 0))
    return pl.pallas_call(
        _ffn_kernel,
        grid=(B, L // tm),
        in_specs=[
            tok, tok,
            pl.BlockSpec((1, HALO, D), lambda b, i: (b, jnp.maximum(i * per - 1, 0), 0)),
            pl.BlockSpec((1, HALO, D), lambda b, i: (b, jnp.minimum((i + 1) * per, nh - 1), 0)),
            _const_spec(wup.shape), _const_spec(cw.shape), _const_spec(cb.shape), _const_spec(wdn.shape),
        ],
        out_specs=tok,
        out_shape=jax.ShapeDtypeStruct((B, L, D), F32),
        scratch_shapes=[pltpu.VMEM((tm + 2 * HALO, D), BF16), pltpu.VMEM((tm, D_FF), BF16)],
        compiler_params=pltpu.CompilerParams(
            dimension_semantics=("arbitrary", "arbitrary"), vmem_limit_bytes=VMEM_LIMIT),
        name="convffn",
    )(h, hn, hn, hn, wup, cw, cb, wdn)


def _rope_tables(L):
    pos = np.arange(L)
    inv = ROPE_THETA ** (-np.arange(0, ROPE_AXIS_DIM, 2, dtype=np.float64) / ROPE_AXIS_DIM)
    ang_r = (pos // GRID_W)[:, None] * inv[None, :]
    ang_c = (pos % GRID_W)[:, None] * inv[None, :]
    ang = np.concatenate([ang_r, ang_r, ang_c, ang_c] * (LANES // AT_HD), axis=-1)
    return jnp.asarray(np.cos(ang), dtype=F32), jnp.asarray(np.sin(ang), dtype=F32)


def _layer(x, norm1_g, w_in, lb_f, lb_b, onorm_g, q_norm_g, k_norm_g, w_a, w_b, w_out, norm2_g,
           w_up, conv_w, conv_b, w_down):
    B, L, D = x.shape
    c_hg = 4 * HG_W
    c_gate = 5 * HG_W
    c_at = c_gate + AT_W + 2 * AT_KV_W
    whg = w_in[:, 0:c_hg].astype(BF16)
    wat = w_in[:, c_gate:c_at].astype(BF16)
    wgh = w_in[:, c_hg:c_gate].astype(BF16)
    wgab = w_in[:, c_at:].astype(BF16)
    n1 = norm1_g.reshape(1, D)
    n2 = norm2_g.reshape(1, D)
    qg2 = jnp.tile(q_norm_g.reshape(1, AT_HD), (1, LANES // AT_HD))
    kg2 = jnp.tile(k_norm_g.reshape(1, AT_HD), (1, LANES // AT_HD))
    cos, sin = _rope_tables(L)
    blk = np.arange(AT_W) // AT_HD
    gsum = jnp.asarray((blk[:, None] == blk[None, :]).astype(np.float32) / AT_HD, dtype=BF16)

    q_bound = LOG2_E * jnp.max(jnp.abs(q_norm_g))
    k_bound = AT_HD ** 0.5 * jnp.max(jnp.abs(k_norm_g))
    score_bound = 1.01 * q_bound * k_bound
    flag = (score_bound <= SCORE_LIMIT).astype(jnp.int32).reshape(1)

    def f8_shift(bound):
        return jnp.clip(jnp.floor(jnp.log2(F8_TARGET_MAX / jnp.maximum(bound, 1e-30))), -60.0, 60.0)

    eq, ek = f8_shift(q_bound), f8_shift(k_bound)
    qs2 = jnp.full((1, LANES), AT_HD ** -0.5 * LOG2_E, F32) * jnp.exp2(eq)
    ks2 = jnp.full((1, LANES), 1.0, F32) * jnp.exp2(ek)
    inv = jnp.exp2(-(eq + ek)).astype(F32).reshape(1)

    hq, hv, gf, gb, aq, ak, vt = _inproj(x, n1, whg, wat, lb_f, lb_b, qg2, kg2, qs2, ks2, cos, sin, gsum)
    o_f, o_b = _hgrn(hq, hv, gf, gb)
    o_at = _attention(flag, inv, aq, ak, vt)
    h, hn = _merge(x, o_f, o_b, o_at, n1, wgh, wgab, onorm_g.reshape(1, HG_DK), w_a.astype(BF16),
                   w_b.astype(BF16), w_out.astype(BF16), n2)
    return _ffn(h, hn, w_up.astype(BF16), conv_w, conv_b.reshape(1, 2 * D_FF), w_down.astype(BF16))


def kernel(x, norm1_g, w_in, hg_lb_fwd, hg_lb_bwd, hg_onorm_g, q_norm_g, k_norm_g, w_branch_a,
           w_branch_b, w_out, norm2_g, w_up, conv_w, conv_b, w_down):
    depth = w_in.shape[0]
    assert depth == 1 and hg_lb_fwd.shape[0] == 2
    l = 0
    return _layer(x, norm1_g[l], w_in[l], hg_lb_fwd, hg_lb_bwd, hg_onorm_g[l], q_norm_g[l], k_norm_g[l],
                  w_branch_a[l], w_branch_b[l], w_out[l], norm2_g[l], w_up[l], conv_w[l], conv_b[l],
                  w_down[l])
```

```python
import functools

import jax
import jax.numpy as jnp
import numpy as np
from jax import lax
from jax.experimental import pallas as pl
from jax.experimental.pallas import tpu as pltpu

F32 = jnp.float32
BF16 = jnp.bfloat16
F8 = jnp.float8_e4m3fn

D_MODEL = 1024
GRID_W = 64
HG_HEADS = 4
HG_DK = 128
HG_W = HG_HEADS * HG_DK
HG_CHUNK = 64
AT_HEADS = 8
AT_KV_HEADS = 2
AT_GROUP = AT_HEADS // AT_KV_HEADS
AT_HD = 64
AT_W = AT_HEADS * AT_HD
AT_KV_W = AT_KV_HEADS * AT_HD
ROPE_THETA = 10000.0
ROPE_AXIS_DIM = AT_HD // 2
D_FF = 2816
EPS = 1e-6
LOG2_E = 1.4426950408889634

LANES = 128
BF16_SUBLANES = 16
F8_TARGET_MAX = 224.0
QK_DEPTH = 4 * AT_HD
VMEM_LIMIT = 56 * 1024 * 1024

VT_ROWS = AT_HD + BF16_SUBLANES
HALO = BF16_SUBLANES

TM_PROJ = 512
HG_SUB = 4 * HG_CHUNK
HG_ROWS = 512
TQ = 512
TK = 512
LOOKAHEAD = 2
LOOKAHEAD_BOUNDED = 2
SCORE_LIMIT = 60.0
KV_BLOCKS_PER_ITER = 16
TM_MERGE = 512
TM_FFN = 512
FF_CHUNK = 256
FF_DOWN_GROUPS = 1


def _dot(a, b):
    return jnp.dot(a, b, preferred_element_type=F32)


def _dot_nt(a, b):
    return lax.dot_general(a, b, (((1,), (1,)), ((), ())), preferred_element_type=F32)


def _dot_tn(a, b):
    return lax.dot_general(a, b, (((0,), (0,)), ((), ())), preferred_element_type=F32)


def _sigmoid(x):
    return 1.0 / (1.0 + jnp.exp(-x))


def _rms_rows(x, g):
    ms = jnp.mean(x * x, axis=-1, keepdims=True)
    return x * lax.rsqrt(ms + EPS) * g


def _const_spec(shape):
    nd = len(shape)
    return pl.BlockSpec(shape, lambda *_: (0,) * nd, pipeline_mode=pl.Buffered(1))


def _lower_bound(lb_ref):
    a0 = lb_ref[0:1, :]
    a1 = lb_ref[1:2, :]
    m = jnp.maximum(a0, a1)
    e0 = jnp.exp(a0 - m)
    e1 = jnp.exp(a1 - m)
    return e0 / (e0 + e1)


def _head_rms(t, gsum):
    sq = t * t
    hi = sq.astype(BF16)
    lo = (sq - hi.astype(F32)).astype(BF16)
    ms = _dot(hi, gsum) + _dot(lo, gsum)
    return t * lax.rsqrt(ms + EPS)


def _rope(t, cos, sin_signed, first_half):
    from_hi = pltpu.roll(t, LANES - ROPE_AXIS_DIM // 2, axis=1)
    from_lo = pltpu.roll(t, ROPE_AXIS_DIM // 2, axis=1)
    return t * cos + jnp.where(first_half, from_hi, from_lo) * sin_signed


def _split_f8(t):
    hi = t.astype(F8).astype(F32)
    return hi, t - hi


def _inproj_kernel(x_ref, n1_ref, whg_ref, wat_ref, lbf_ref, lbb_ref, qg_ref, kg_ref, qs_ref, ks_ref,
                   cos_ref, sin_ref, gsum_ref,
                   hq_ref, hv_ref, gf_ref, gb_ref, aq_ref, ak_ref, vt_ref):
    x = x_ref[0]
    tm = x.shape[0]
    u = _rms_rows(x, n1_ref[...]).astype(BF16)

    cos = cos_ref[...]
    sin = sin_ref[...]
    lane = lax.broadcasted_iota(jnp.int32, cos.shape, 1)
    first_half = (lane & (ROPE_AXIS_DIM - 1)) < (ROPE_AXIS_DIM // 2)
    sin_signed = jnp.where(first_half, -sin, sin)
    gsum = gsum_ref[...]

    def silu_q(hq):
        hq_ref[0] = hq * _sigmoid(hq)

    def values(hv):
        hv_ref[0] = hv.astype(BF16)

    def log_forget(out_ref, lb_ref):
        def epilogue(pre):
            lb = _lower_bound(lb_ref)
            out_ref[0] = jnp.log(lb + (1.0 - lb) * _sigmoid(pre))
        return epilogue

    def attn_q(aq):
        qn = _head_rms(aq, gsum)
        for c in range(AT_W // LANES):
            t = qn[:, c * LANES:(c + 1) * LANES] * qg_ref[...]
            hi, lo = _split_f8((_rope(t, cos, sin_signed, first_half) * qs_ref[...]).T)
            for hh in range(LANES // AT_HD):
                rows = slice(hh * AT_HD, (hh + 1) * AT_HD)
                head = c * (LANES // AT_HD) + hh
                aq_ref[0, head * QK_DEPTH:(head + 1) * QK_DEPTH, :] = jnp.concatenate(
                    [hi[rows], lo[rows], hi[rows], lo[rows]], axis=0).astype(F8)

    def attn_k(ak):
        kn = _head_rms(ak, gsum[0:LANES, 0:LANES]) * kg_ref[...]
        hi, lo = _split_f8(_rope(kn, cos, sin_signed, first_half) * ks_ref[...])
        for g in range(AT_KV_HEADS):
            cols = slice(g * AT_HD, (g + 1) * AT_HD)
            ak_ref[0, g] = jnp.concatenate([hi[:, cols], hi[:, cols], lo[:, cols], lo[:, cols]],
                                           axis=1).astype(F8)

    def attn_v(av):
        avt = av.T
        row = lax.broadcasted_iota(jnp.int32, (VT_ROWS - AT_HD, tm), 0)
        aug = jnp.where(row == 0, 1.0, 0.0).astype(BF16)
        for g in range(AT_KV_HEADS):
            vt_ref[0, g, 0:AT_HD, :] = avt[g * AT_HD:(g + 1) * AT_HD].astype(BF16)
            vt_ref[0, g, AT_HD:VT_ROWS, :] = aug

    sections = (
        (wat_ref, 0, AT_W, attn_q),
        (wat_ref, AT_W, AT_KV_W, attn_k),
        (wat_ref, AT_W + AT_KV_W, AT_KV_W, attn_v),
        (whg_ref, 2 * HG_W, HG_W, log_forget(gf_ref, lbf_ref)),
        (whg_ref, 3 * HG_W, HG_W, log_forget(gb_ref, lbb_ref)),
        (whg_ref, 0, HG_W, silu_q),
        (whg_ref, HG_W, HG_W, values),
    )

    def project(s):
        w_ref, off, width, _ = sections[s]
        return _dot(u, w_ref[:, off:off + width])

    nxt = project(0)
    for s in range(len(sections)):
        cur = nxt
        if s + 1 < len(sections):
            nxt = project(s + 1)
        sections[s][3](cur)


def _inproj(x, n1, whg, wat, lbf, lbb, qg2, kg2, qs2, ks2, cos, sin, gsum):
    B, L, D = x.shape
    tm = min(TM_PROJ, L)
    nt = L // tm
    tok = lambda w: pl.BlockSpec((1, tm, w), lambda b, i: (b, i, 0))
    out_shape = (
        jax.ShapeDtypeStruct((B, L, HG_W), F32),
        jax.ShapeDtypeStruct((B, L, HG_W), BF16),
        jax.ShapeDtypeStruct((B, L, HG_W), F32),
        jax.ShapeDtypeStruct((B, L, HG_W), F32),
        jax.ShapeDtypeStruct((B, AT_HEADS * QK_DEPTH, L), F8),
        jax.ShapeDtypeStruct((B, AT_KV_HEADS, L, QK_DEPTH), F8),
        jax.ShapeDtypeStruct((B, AT_KV_HEADS, VT_ROWS, L), BF16),
    )
    out_specs = (
        tok(HG_W), tok(HG_W), tok(HG_W), tok(HG_W),
        pl.BlockSpec((1, AT_HEADS * QK_DEPTH, tm), lambda b, i: (b, 0, i)),
        pl.BlockSpec((1, AT_KV_HEADS, tm, QK_DEPTH), lambda b, i: (b, 0, i, 0)),
        pl.BlockSpec((1, AT_KV_HEADS, VT_ROWS, tm), lambda b, i: (b, 0, 0, i)),
    )
    in_specs = [
        tok(D), _const_spec(n1.shape), _const_spec(whg.shape), _const_spec(wat.shape),
        _const_spec(lbf.shape), _const_spec(lbb.shape), _const_spec(qg2.shape), _const_spec(kg2.shape),
        _const_spec(qs2.shape), _const_spec(ks2.shape),
        pl.BlockSpec((tm, LANES), lambda b, i: (i, 0)),
        pl.BlockSpec((tm, LANES), lambda b, i: (i, 0)),
        _const_spec(gsum.shape),
    ]
    return pl.pallas_call(
        _inproj_kernel,
        grid=(B, nt),
        in_specs=in_specs,
        out_specs=out_specs,
        out_shape=out_shape,
        compiler_params=pltpu.CompilerParams(
            dimension_semantics=("arbitrary", "arbitrary"), vmem_limit_bytes=VMEM_LIMIT),
        name="inproj",
    )(x, n1, whg, wat, lbf, lbb, qg2, kg2, qs2, ks2, cos, sin, gsum)


def _chunk_mask(n, reverse):
    shift = HG_CHUNK.bit_length() - 1
    r = lax.broadcasted_iota(jnp.int32, (n, n), 0)
    c = lax.broadcasted_iota(jnp.int32, (n, n), 1)
    same = lax.shift_right_logical(r, shift) == lax.shift_right_logical(c, shift)
    return same & ((c >= r) if reverse else (c <= r))


def _hgrn_kernel(qf_ref, vf_ref, gf_ref, qb_ref, vb_ref, gb_ref, of_ref, ob_ref, stf_ref, stb_ref):
    @pl.when(pl.program_id(1) == 0)
    def _():
        stf_ref[...] = jnp.zeros_like(stf_ref)
        stb_ref[...] = jnp.zeros_like(stb_ref)

    n = HG_SUB
    nsub = qf_ref.shape[1] // n
    nchunk = n // HG_CHUNK
    tri = {rev: _chunk_mask(n, rev) for rev in (False, True)}

    groups = []
    for h in range(HG_HEADS):
        cols = slice(h * HG_DK, (h + 1) * HG_DK)
        for s in range(nsub):
            groups.append(((qf_ref, vf_ref, gf_ref, of_ref), False, h, slice(s * n, (s + 1) * n), cols))
        for s in range(nsub - 1, -1, -1):
            groups.append(((qb_ref, vb_ref, gb_ref, ob_ref), True, h, slice(s * n, (s + 1) * n), cols))

    gs = [refs[2][0, rs, cs] for refs, _, _, rs, cs in groups]
    pos = lax.broadcasted_iota(jnp.int32, (n, HG_DK), 0) & (HG_CHUNK - 1)
    bs = []
    for g, (_, rev, _, _, _) in zip(gs, groups):
        b = g
        step = 1
        while step < HG_CHUNK:
            if rev:
                b = b + jnp.where(pos < HG_CHUNK - step, pltpu.roll(b, n - step, axis=0), 0.0)
            else:
                b = b + jnp.where(pos >= step, pltpu.roll(b, step, axis=0), 0.0)
            step *= 2
        bs.append(b)

    qds, kds, kcs, decs = [], [], [], []
    for g, b, (refs, rev, _, rs, cs) in zip(gs, bs, groups):
        b3 = b.reshape(nchunk, HG_CHUNK, HG_DK)
        tot = b3[:, 0:1, :] if rev else b3[:, HG_CHUNK - 1:HG_CHUNK, :]
        totf = jnp.broadcast_to(tot, b3.shape).reshape(n, HG_DK)
        k = 1.0 - jnp.exp(g)
        qds.append((refs[0][0, rs, cs] * jnp.exp(b)).astype(BF16))
        kds.append((k * jnp.exp(-b)).astype(BF16))
        kcs.append((k * jnp.exp(totf - b)).astype(BF16))
        decs.append(jnp.exp(tot))

    amats = [_dot_nt(qd, kd) for qd, kd in zip(qds, kds)]
    vs = [refs[1][0, rs, cs] for refs, _, _, rs, cs in groups]
    intras = [_dot(jnp.where(tri[rev], a, 0.0).astype(BF16), v)
              for a, v, (_, rev, _, _, _) in zip(amats, vs, groups)]
    chunks = [slice(c * HG_CHUNK, (c + 1) * HG_CHUNK) for c in range(nchunk)]
    uts = [[_dot_tn(v[sl], kc[sl]) for sl in chunks] for v, kc in zip(vs, kcs)]

    chains = {}
    for gi, (_, rev, h, _, _) in enumerate(groups):
        order = range(nchunk - 1, -1, -1) if rev else range(nchunk)
        chains.setdefault((rev, h), []).extend((gi, ci) for ci in order)
    states = {key: (stb_ref if key[0] else stf_ref)[key[1]] for key in chains}
    outs = [[None] * nchunk for _ in groups]
    for t in range(nsub * nchunk):
        for key, steps in chains.items():
            gi, ci = steps[t]
            st = states[key]
            outs[gi][ci] = intras[gi][chunks[ci]] + _dot_nt(qds[gi][chunks[ci]], st.astype(BF16))
            states[key] = st * decs[gi][ci] + uts[gi][ci]
    for gi, (refs, _, _, rs, cs) in enumerate(groups):
        refs[3][0, rs, cs] = jnp.concatenate(outs[gi], axis=0)
    for (rev, h), st in states.items():
        (stb_ref if rev else stf_ref)[h] = st


def _hgrn(hq, hv, gf, gb):
    B, L, _ = hq.shape
    rows = min(HG_ROWS, L)
    nb = L // rows
    fwd = pl.BlockSpec((1, rows, HG_W), lambda b, i: (b, i, 0))
    bwd = pl.BlockSpec((1, rows, HG_W), lambda b, i: (b, nb - 1 - i, 0))
    o = jax.ShapeDtypeStruct((B, L, HG_W), F32)
    state = pltpu.VMEM((HG_HEADS, HG_DK, HG_DK), F32)
    return pl.pallas_call(
        _hgrn_kernel,
        grid=(B, nb),
        in_specs=[fwd, fwd, fwd, bwd, bwd, bwd],
        out_specs=(fwd, bwd),
        out_shape=(o, o),
        scratch_shapes=[state, state],
        compiler_params=pltpu.CompilerParams(
            dimension_semantics=("arbitrary", "arbitrary"), vmem_limit_bytes=VMEM_LIMIT),
        name="hgrn2",
    )(hq, hv, gf, hq, hv, gb)


def _attn_kernel(flag_ref, inv_ref, q_ref, k_ref, vt_ref, o_ref, acc_ref, *, tk, inner):
    tq = q_ref.shape[2]
    nk = k_ref.shape[2] // tk
    qts = [q_ref[0, j * QK_DEPTH:(j + 1) * QK_DEPTH, :] for j in range(AT_GROUP)]
    inv = inv_ref[0]

    def scores(blk, j, stabilised):
        start = pl.multiple_of(blk * tk, tk)
        st = _dot(k_ref[0, 0, pl.ds(start, tk), :], qts[j]) * inv
        return st if stabilised else jnp.exp2(st).astype(BF16)

    def run(stabilised, lookahead):
        def body(it, carry):
            pending, state = carry
            pending = list(pending)
            state = list(state)
            for n in range(inner * AT_GROUP):
                blk = it * inner + n // AT_GROUP
                j = n % AT_GROUP
                start = pl.multiple_of(blk * tk, tk)
                vb = vt_ref[0, 0, :, pl.ds(start, tk)]
                st = pending.pop(0)
                ahead = n + lookahead
                pending.append(scores(jnp.minimum(it * inner + ahead // AT_GROUP, nk - 1), ahead % AT_GROUP,
                                      stabilised))
                if stabilised:
                    m, acc = state[j]
                    m_new = jnp.maximum(m, jnp.max(st, axis=0, keepdims=True))
                    p = jnp.exp2(st - m_new).astype(BF16)
                    state[j] = (m_new, jnp.exp2(m - m_new) * acc + _dot(vb, p))
                else:
                    state[j] = state[j] + _dot(vb, st)
            return tuple(pending), tuple(state)

        zero = jnp.zeros((VT_ROWS, tq), F32)
        init = tuple((jnp.full((1, tq), -1e30, F32), zero) if stabilised else zero for _ in range(AT_GROUP))
        first = tuple(scores(min(n // AT_GROUP, nk - 1), n % AT_GROUP, stabilised) for n in range(lookahead))
        _, final = lax.fori_loop(0, nk // inner, body, (first, init))
        for j in range(AT_GROUP):
            acc_ref[j] = final[j][1] if stabilised else final[j]

    @pl.when(flag_ref[0] != 0)
    def _():
        run(False, LOOKAHEAD_BOUNDED)

    @pl.when(flag_ref[0] == 0)
    def _():
        run(True, LOOKAHEAD)

    for j in range(AT_GROUP):
        o_ref[0, j * AT_HD:(j + 1) * AT_HD, :] = (
            acc_ref[j, 0:AT_HD, :] / acc_ref[j, AT_HD:AT_HD + 1, :]).astype(o_ref.dtype)


def _attention(flag, inv, aq, ak, vt):
    B, _, L = aq.shape
    tq = min(TQ, L)
    tk = min(TK, L)
    gw = AT_GROUP * AT_HD
    return pl.pallas_call(
        functools.partial(_attn_kernel, tk=tk, inner=min(KV_BLOCKS_PER_ITER, L // tk)),
        grid=(B, AT_KV_HEADS, L // tq),
        in_specs=[
            pl.BlockSpec(memory_space=pltpu.SMEM),
            pl.BlockSpec(memory_space=pltpu.SMEM),
            pl.BlockSpec((1, AT_GROUP * QK_DEPTH, tq), lambda b, g, i: (b, g, i)),
            pl.BlockSpec((1, 1, L, QK_DEPTH), lambda b, g, i: (b, g, 0, 0)),
            pl.BlockSpec((1, 1, VT_ROWS, L), lambda b, g, i: (b, g, 0, 0)),
        ],
        out_specs=pl.BlockSpec((1, gw, tq), lambda b, g, i: (b, g, i)),
        out_shape=jax.ShapeDtypeStruct((B, AT_W, L), BF16),
        scratch_shapes=[pltpu.VMEM((AT_GROUP, VT_ROWS, tq), F32)],
        compiler_params=pltpu.CompilerParams(
            dimension_semantics=("arbitrary", "arbitrary", "arbitrary"), vmem_limit_bytes=VMEM_LIMIT),
        name="attention",
    )(flag, inv, aq, ak, vt)


def _merge_kernel(x_ref, of_ref, ob_ref, oat_ref, n1_ref, wgh_ref, wgab_ref, ong_ref, wa_ref, wb_ref,
                  wo_ref, n2_ref, h_ref, hn_ref):
    x = x_ref[0]
    u = _rms_rows(x, n1_ref[...]).astype(BF16)
    hgate = _dot(u, wgh_ref[...])
    yb = _dot_tn(oat_ref[0], wb_ref[...])
    gb = _dot(u, wgab_ref[:, D_MODEL:2 * D_MODEL])
    o = of_ref[0] + ob_ref[0]
    heads = []
    for h in range(HG_HEADS):
        sl = slice(h * HG_DK, (h + 1) * HG_DK)
        heads.append(_rms_rows(o[:, sl], ong_ref[...]))
    oa = (jnp.concatenate(heads, axis=1) * (hgate * _sigmoid(hgate))).astype(BF16)
    ga = _dot(u, wgab_ref[:, 0:D_MODEL])
    gated_b = _sigmoid(gb) * yb
    ya = _dot(oa, wa_ref[...])
    merged = (_sigmoid(ga) * ya + gated_b).astype(BF16)
    h = x + _dot(merged, wo_ref[...])
    h_ref[0] = h
    hn_ref[0] = _rms_rows(h, n2_ref[...]).astype(BF16)


def _merge(x, o_f, o_b, o_at, n1, wgh, wgab, ong, wa, wb, wo, n2):
    B, L, D = x.shape
    tm = min(TM_MERGE, L)
    tok = lambda w: pl.BlockSpec((1, tm, w), lambda b, i: (b, i, 0))
    return pl.pallas_call(
        _merge_kernel,
        grid=(B, L // tm),
        in_specs=[tok(D), tok(HG_W), tok(HG_W), pl.BlockSpec((1, AT_W, tm), lambda b, i: (b, 0, i)),
                  _const_spec(n1.shape), _const_spec(wgh.shape),
                  _const_spec(wgab.shape), _const_spec(ong.shape), _const_spec(wa.shape),
                  _const_spec(wb.shape), _const_spec(wo.shape), _const_spec(n2.shape)],
        out_specs=(tok(D), tok(D)),
        out_shape=(jax.ShapeDtypeStruct((B, L, D), F32), jax.ShapeDtypeStruct((B, L, D), BF16)),
        compiler_params=pltpu.CompilerParams(
            dimension_semantics=("arbitrary", "arbitrary"), vmem_limit_bytes=VMEM_LIMIT),
        name="merge",
    )(x, o_f, o_b, o_at, n1, wgh, wgab, ong, wa, wb, wo, n2)


def _ffn_kernel(h_ref, hn_ref, prev_ref, next_ref, wup_ref, cw_ref, cb_ref, wdn_ref, y_ref, ext_ref,
                act_ref):
    i = pl.program_id(1)
    nt = pl.num_programs(1)
    tm = hn_ref.shape[1]
    ext = tm + 2 * HALO
    ext_ref[0:HALO, :] = jnp.where(i > 0, prev_ref[0], jnp.zeros_like(prev_ref[0]))
    ext_ref[HALO:HALO + tm, :] = hn_ref[0]
    ext_ref[HALO + tm:ext, :] = jnp.where(i < nt - 1, next_ref[0], jnp.zeros_like(next_ref[0]))
    hx = ext_ref[...]

    def conv(t, off):
        w = cw_ref[:, off:off + FF_CHUNK]
        before = pltpu.roll(t, 1, axis=0)[HALO:HALO + tm]
        after = pltpu.roll(t, ext - 1, axis=0)[HALO:HALO + tm]
        return (before * w[0:1] + t[HALO:HALO + tm] * w[1:2] + after * w[2:3]
                + cb_ref[:, off:off + FF_CHUNK])

    def up(j):
        off = j * FF_CHUNK
        return (_dot(hx, wup_ref[:, off:off + FF_CHUNK]),
                _dot(hx, wup_ref[:, D_FF + off:D_FF + off + FF_CHUNK]))

    nchunk = D_FF // FF_CHUNK
    bounds = [round(g * nchunk / FF_DOWN_GROUPS) * FF_CHUNK for g in range(FF_DOWN_GROUPS + 1)]
    y = h_ref[0]
    nxt = up(0)
    for j in range(nchunk):
        off = j * FF_CHUNK
        upv, upg = nxt
        if j + 1 < nchunk:
            nxt = up(j + 1)
        val = conv(upv, off)
        gate = conv(upg, D_FF + off)
        act_ref[:, off:off + FF_CHUNK] = (gate * _sigmoid(gate) * val).astype(BF16)
        if off + FF_CHUNK in bounds[1:]:
            lo = bounds[bounds.index(off + FF_CHUNK) - 1]
            y = y + _dot(act_ref[:, lo:off + FF_CHUNK], wdn_ref[lo:off + FF_CHUNK, :])
    y_ref[0] = y


def _ffn(h, hn, wup, cw, cb, wdn):
    B, L, D = h.shape
    tm = min(TM_FFN, L)
    per = tm // HALO
    nh = L // HALO
    tok = pl.BlockSpec((1, tm, D), lambda b, i: (b, i, 0))
    return pl.pallas_call(
        _ffn_kernel,
        grid=(B, L // tm),
        in_specs=[
            tok, tok,
            pl.BlockSpec((1, HALO, D), lambda b, i: (b, jnp.maximum(i * per - 1, 0), 0)),
            pl.BlockSpec((1, HALO, D), lambda b, i: (b, jnp.minimum((i + 1) * per, nh - 1), 0)),
            _const_spec(wup.shape), _const_spec(cw.shape), _const_spec(cb.shape), _const_spec(wdn.shape),
        ],
        out_specs=tok,
        out_shape=jax.ShapeDtypeStruct((B, L, D), F32),
        scratch_shapes=[pltpu.VMEM((tm + 2 * HALO, D), BF16), pltpu.VMEM((tm, D_FF), BF16)],
        compiler_params=pltpu.CompilerParams(
            dimension_semantics=("arbitrary", "arbitrary"), vmem_limit_bytes=VMEM_LIMIT),
        name="convffn",
    )(h, hn, hn, hn, wup, cw, cb, wdn)


def _rope_tables(L):
    pos = np.arange(L)
    inv = ROPE_THETA ** (-np.arange(0, ROPE_AXIS_DIM, 2, dtype=np.float64) / ROPE_AXIS_DIM)
    ang_r = (pos // GRID_W)[:, None] * inv[None, :]
    ang_c = (pos % GRID_W)[:, None] * inv[None, :]
    ang = np.concatenate([ang_r, ang_r, ang_c, ang_c] * (LANES // AT_HD), axis=-1)
    return jnp.asarray(np.cos(ang), dtype=F32), jnp.asarray(np.sin(ang), dtype=F32)


def _layer(x, norm1_g, w_in, lb_f, lb_b, onorm_g, q_norm_g, k_norm_g, w_a, w_b, w_out, norm2_g,
           w_up, conv_w, conv_b, w_down):
    B, L, D = x.shape
    c_hg = 4 * HG_W
    c_gate = 5 * HG_W
    c_at = c_gate + AT_W + 2 * AT_KV_W
    whg = w_in[:, 0:c_hg].astype(BF16)
    wat = w_in[:, c_gate:c_at].astype(BF16)
    wgh = w_in[:, c_hg:c_gate].astype(BF16)
    wgab = w_in[:, c_at:].astype(BF16)
    n1 = norm1_g.reshape(1, D)
    n2 = norm2_g.reshape(1, D)
    qg2 = jnp.tile(q_norm_g.reshape(1, AT_HD), (1, LANES // AT_HD))
    kg2 = jnp.tile(k_norm_g.reshape(1, AT_HD), (1, LANES // AT_HD))
    cos, sin = _rope_tables(L)
    blk = np.arange(AT_W) // AT_HD
    gsum = jnp.asarray((blk[:, None] == blk[None, :]).astype(np.float32) / AT_HD, dtype=BF16)

    q_bound = LOG2_E * jnp.max(jnp.abs(q_norm_g))
    k_bound = AT_HD ** 0.5 * jnp.max(jnp.abs(k_norm_g))
    score_bound = 1.01 * q_bound * k_bound
    flag = (score_bound <= SCORE_LIMIT).astype(jnp.int32).reshape(1)

    def f8_shift(bound):
        return jnp.clip(jnp.floor(jnp.log2(F8_TARGET_MAX / jnp.maximum(bound, 1e-30))), -60.0, 60.0)

    eq, ek = f8_shift(q_bound), f8_shift(k_bound)
    qs2 = jnp.full((1, LANES), AT_HD ** -0.5 * LOG2_E, F32) * jnp.exp2(eq)
    ks2 = jnp.full((1, LANES), 1.0, F32) * jnp.exp2(ek)
    inv = jnp.exp2(-(eq + ek)).astype(F32).reshape(1)

    hq, hv, gf, gb, aq, ak, vt = _inproj(x, n1, whg, wat, lb_f, lb_b, qg2, kg2, qs2, ks2, cos, sin, gsum)
    o_f, o_b = _hgrn(hq, hv, gf, gb)
    o_at = _attention(flag, inv, aq, ak, vt)
    h, hn = _merge(x, o_f, o_b, o_at, n1, wgh, wgab, onorm_g.reshape(1, HG_DK), w_a.astype(BF16),
                   w_b.astype(BF16), w_out.astype(BF16), n2)
    return _ffn(h, hn, w_up.astype(BF16), conv_w, conv_b.reshape(1, 2 * D_FF), w_down.astype(BF16))


def kernel(x, norm1_g, w_in, hg_lb_fwd, hg_lb_bwd, hg_onorm_g, q_norm_g, k_norm_g, w_branch_a,
           w_branch_b, w_out, norm2_g, w_up, conv_w, conv_b, w_down):
    depth = w_in.shape[0]
    assert depth == 1 and hg_lb_fwd.shape[0] == 2
    l = 0
    return _layer(x, norm1_g[l], w_in[l], hg_lb_fwd, hg_lb_bwd, hg_onorm_g[l], q_norm_g[l], k_norm_g[l],
                  w_branch_a[l], w_branch_b[l], w_out[l], norm2_g[l], w_up[l], conv_w[l], conv_b[l],
                  w_down[l])
```

```python
import functools

import jax
import jax.numpy as jnp
import numpy as np
from jax import lax
from jax.experimental import pallas as pl
from jax.experimental.pallas import tpu as pltpu

F32 = jnp.float32
BF16 = jnp.bfloat16
F8 = jnp.float8_e4m3fn

D_MODEL = 1024
GRID_W = 64
HG_HEADS = 4
HG_DK = 128
HG_W = HG_HEADS * HG_DK
HG_CHUNK = 64
AT_HEADS = 8
AT_KV_HEADS = 2
AT_GROUP = AT_HEADS // AT_KV_HEADS
AT_HD = 64
AT_W = AT_HEADS * AT_HD
AT_KV_W = AT_KV_HEADS * AT_HD
ROPE_THETA = 10000.0
ROPE_AXIS_DIM = AT_HD // 2
D_FF = 2816
EPS = 1e-6
LOG2_E = 1.4426950408889634

LANES = 128
BF16_SUBLANES = 16
F8_TARGET_MAX = 224.0
QK_DEPTH = 4 * AT_HD
VMEM_LIMIT = 56 * 1024 * 1024

VT_ROWS = AT_HD + BF16_SUBLANES
HALO = BF16_SUBLANES

TM_PROJ = 512
HG_SUB = 4 * HG_CHUNK
HG_ROWS = 512
TQ = 512
TK = 512
LOOKAHEAD = 2
LOOKAHEAD_BOUNDED = 2
SCORE_LIMIT = 60.0
KV_BLOCKS_PER_ITER = 16
TM_MERGE = 512
TM_FFN = 512
FF_CHUNK = 256
FF_DOWN_GROUPS = 1


def _dot(a, b):
    return jnp.dot(a, b, preferred_element_type=F32)


def _dot_nt(a, b):
    return lax.dot_general(a, b, (((1,), (1,)), ((), ())), preferred_element_type=F32)


def _dot_tn(a, b):
    return lax.dot_general(a, b, (((0,), (0,)), ((), ())), preferred_element_type=F32)


def _sigmoid(x):
    return 1.0 / (1.0 + jnp.exp(-x))


def _rms_rows(x, g):
    ms = jnp.mean(x * x, axis=-1, keepdims=True)
    return x * lax.rsqrt(ms + EPS) * g


def _const_spec(shape):
    nd = len(shape)
    return pl.BlockSpec(shape, lambda *_: (0,) * nd, pipeline_mode=pl.Buffered(1))


def _cast_plan(weights, batch, tiles):
    steps = batch * tiles
    specs, types = [], []
    for w in weights:
        rows = w.shape[0]
        nblk = steps
        while rows % nblk or (rows // nblk) % BF16_SUBLANES:
            nblk //= 2
        specs.append(pl.BlockSpec((rows // nblk, w.shape[1]),
                                  lambda b, i, nblk=nblk: ((b * tiles + i) * nblk // steps, 0)))
        types.append(jax.ShapeDtypeStruct(w.shape, BF16))
    return specs, types


def _cast_blocks(src_refs, dst_refs):
    for src, dst in zip(src_refs, dst_refs):
        dst[...] = src[...].astype(BF16)


def _lower_bound(lb_ref):
    a0 = lb_ref[0:1, :]
    a1 = lb_ref[1:2, :]
    m = jnp.maximum(a0, a1)
    e0 = jnp.exp(a0 - m)
    e1 = jnp.exp(a1 - m)
    return e0 / (e0 + e1)


def _head_rms(t, gsum):
    sq = t * t
    hi = sq.astype(BF16)
    lo = (sq - hi.astype(F32)).astype(BF16)
    ms = _dot(hi, gsum) + _dot(lo, gsum)
    return t * lax.rsqrt(ms + EPS)


def _rope(t, cos, sin_signed, first_half):
    from_hi = pltpu.roll(t, LANES - ROPE_AXIS_DIM // 2, axis=1)
    from_lo = pltpu.roll(t, ROPE_AXIS_DIM // 2, axis=1)
    return t * cos + jnp.where(first_half, from_hi, from_lo) * sin_signed


def _split_f8(t):
    hi = t.astype(F8).astype(F32)
    return hi, t - hi


def _inproj_kernel(x_ref, n1_ref, whg_ref, wat_ref, lbf_ref, lbb_ref, qg_ref, kg_ref, qs_ref, ks_ref,
                   cos_ref, sin_ref, gsum_ref, *rest, ncast):
    cast_src, rest = rest[:ncast], rest[ncast:]
    hq_ref, hv_ref, gf_ref, gb_ref, aq_ref, ak_ref, vt_ref = rest[:7]
    _cast_blocks(cast_src, rest[7:])
    x = x_ref[0]
    tm = x.shape[0]
    u = _rms_rows(x, n1_ref[...]).astype(BF16)

    cos = cos_ref[...]
    sin = sin_ref[...]
    lane = lax.broadcasted_iota(jnp.int32, cos.shape, 1)
    first_half = (lane & (ROPE_AXIS_DIM - 1)) < (ROPE_AXIS_DIM // 2)
    sin_signed = jnp.where(first_half, -sin, sin)
    gsum = gsum_ref[...]

    def silu_q(hq):
        hq_ref[0] = hq * _sigmoid(hq)

    def values(hv):
        hv_ref[0] = hv.astype(BF16)

    def log_forget(out_ref, lb_ref):
        def epilogue(pre):
            lb = _lower_bound(lb_ref)
            out_ref[0] = jnp.log(lb + (1.0 - lb) * _sigmoid(pre))
        return epilogue

    def attn_q(aq):
        qn = _head_rms(aq, gsum)
        for c in range(AT_W // LANES):
            t = qn[:, c * LANES:(c + 1) * LANES] * qg_ref[...]
            hi, lo = _split_f8((_rope(t, cos, sin_signed, first_half) * qs_ref[...]).T)
            for hh in range(LANES // AT_HD):
                rows = slice(hh * AT_HD, (hh + 1) * AT_HD)
                head = c * (LANES // AT_HD) + hh
                aq_ref[0, head * QK_DEPTH:(head + 1) * QK_DEPTH, :] = jnp.concatenate(
                    [hi[rows], lo[rows], hi[rows], lo[rows]], axis=0).astype(F8)

    def attn_k(ak):
        kn = _head_rms(ak, gsum[0:LANES, 0:LANES]) * kg_ref[...]
        hi, lo = _split_f8(_rope(kn, cos, sin_signed, first_half) * ks_ref[...])
        for g in range(AT_KV_HEADS):
            cols = slice(g * AT_HD, (g + 1) * AT_HD)
            ak_ref[0, g] = jnp.concatenate([hi[:, cols], hi[:, cols], lo[:, cols], lo[:, cols]],
                                           axis=1).astype(F8)

    def attn_v(av):
        avt = av.T
        row = lax.broadcasted_iota(jnp.int32, (VT_ROWS - AT_HD, tm), 0)
        aug = jnp.where(row == 0, 1.0, 0.0).astype(BF16)
        for g in range(AT_KV_HEADS):
            vt_ref[0, g, 0:AT_HD, :] = avt[g * AT_HD:(g + 1) * AT_HD].astype(BF16)
            vt_ref[0, g, AT_HD:VT_ROWS, :] = aug

    sections = (
        (wat_ref, 0, AT_W, attn_q),
        (wat_ref, AT_W, AT_KV_W, attn_k),
        (wat_ref, AT_W + AT_KV_W, AT_KV_W, attn_v),
        (whg_ref, 2 * HG_W, HG_W, log_forget(gf_ref, lbf_ref)),
        (whg_ref, 3 * HG_W, HG_W, log_forget(gb_ref, lbb_ref)),
        (whg_ref, 0, HG_W, silu_q),
        (whg_ref, HG_W, HG_W, values),
    )

    def project(s):
        w_ref, off, width, _ = sections[s]
        return _dot(u, w_ref[:, off:off + width])

    nxt = project(0)
    for s in range(len(sections)):
        cur = nxt
        if s + 1 < len(sections):
            nxt = project(s + 1)
        sections[s][3](cur)


def _inproj(x, n1, whg, wat, lbf, lbb, qg2, kg2, qs2, ks2, cos, sin, gsum, later_weights):
    B, L, D = x.shape
    tm = min(TM_PROJ, L)
    nt = L // tm
    cast_specs, cast_types = _cast_plan(later_weights, B, nt)
    tok = lambda w: pl.BlockSpec((1, tm, w), lambda b, i: (b, i, 0))
    out_shape = (
        jax.ShapeDtypeStruct((B, L, HG_W), F32),
        jax.ShapeDtypeStruct((B, L, HG_W), BF16),
        jax.ShapeDtypeStruct((B, L, HG_W), F32),
        jax.ShapeDtypeStruct((B, L, HG_W), F32),
        jax.ShapeDtypeStruct((B, AT_HEADS * QK_DEPTH, L), F8),
        jax.ShapeDtypeStruct((B, AT_KV_HEADS, L, QK_DEPTH), F8),
        jax.ShapeDtypeStruct((B, AT_KV_HEADS, VT_ROWS, L), BF16),
    )
    out_specs = (
        tok(HG_W), tok(HG_W), tok(HG_W), tok(HG_W),
        pl.BlockSpec((1, AT_HEADS * QK_DEPTH, tm), lambda b, i: (b, 0, i)),
        pl.BlockSpec((1, AT_KV_HEADS, tm, QK_DEPTH), lambda b, i: (b, 0, i, 0)),
        pl.BlockSpec((1, AT_KV_HEADS, VT_ROWS, tm), lambda b, i: (b, 0, 0, i)),
    )
    in_specs = [
        tok(D), _const_spec(n1.shape), _const_spec(whg.shape), _const_spec(wat.shape),
        _const_spec(lbf.shape), _const_spec(lbb.shape), _const_spec(qg2.shape), _const_spec(kg2.shape),
        _const_spec(qs2.shape), _const_spec(ks2.shape),
        pl.BlockSpec((tm, LANES), lambda b, i: (i, 0)),
        pl.BlockSpec((tm, LANES), lambda b, i: (i, 0)),
        _const_spec(gsum.shape),
    ]
    return pl.pallas_call(
        functools.partial(_inproj_kernel, ncast=len(later_weights)),
        grid=(B, nt),
        in_specs=in_specs + cast_specs,
        out_specs=out_specs + tuple(cast_specs),
        out_shape=out_shape + tuple(cast_types),
        compiler_params=pltpu.CompilerParams(
            dimension_semantics=("arbitrary", "arbitrary"), vmem_limit_bytes=VMEM_LIMIT),
        name="inproj",
    )(x, n1, whg, wat, lbf, lbb, qg2, kg2, qs2, ks2, cos, sin, gsum, *later_weights)


def _chunk_mask(n, reverse):
    shift = HG_CHUNK.bit_length() - 1
    r = lax.broadcasted_iota(jnp.int32, (n, n), 0)
    c = lax.broadcasted_iota(jnp.int32, (n, n), 1)
    same = lax.shift_right_logical(r, shift) == lax.shift_right_logical(c, shift)
    return same & ((c >= r) if reverse else (c <= r))


def _hgrn_kernel(qf_ref, vf_ref, gf_ref, qb_ref, vb_ref, gb_ref, of_ref, ob_ref, stf_ref, stb_ref):
    @pl.when(pl.program_id(1) == 0)
    def _():
        stf_ref[...] = jnp.zeros_like(stf_ref)
        stb_ref[...] = jnp.zeros_like(stb_ref)

    n = HG_SUB
    nsub = qf_ref.shape[1] // n
    nchunk = n // HG_CHUNK
    tri = {rev: _chunk_mask(n, rev) for rev in (False, True)}

    groups = []
    for h in range(HG_HEADS):
        cols = slice(h * HG_DK, (h + 1) * HG_DK)
        for s in range(nsub):
            groups.append(((qf_ref, vf_ref, gf_ref, of_ref), False, h, slice(s * n, (s + 1) * n), cols))
        for s in range(nsub - 1, -1, -1):
            groups.append(((qb_ref, vb_ref, gb_ref, ob_ref), True, h, slice(s * n, (s + 1) * n), cols))

    gs = [refs[2][0, rs, cs] for refs, _, _, rs, cs in groups]
    pos = lax.broadcasted_iota(jnp.int32, (n, HG_DK), 0) & (HG_CHUNK - 1)
    bs = []
    for g, (_, rev, _, _, _) in zip(gs, groups):
        b = g
        step = 1
        while step < HG_CHUNK:
            if rev:
                b = b + jnp.where(pos < HG_CHUNK - step, pltpu.roll(b, n - step, axis=0), 0.0)
            else:
                b = b + jnp.where(pos >= step, pltpu.roll(b, step, axis=0), 0.0)
            step *= 2
        bs.append(b)

    qds, kds, kcs, decs = [], [], [], []
    for g, b, (refs, rev, _, rs, cs) in zip(gs, bs, groups):
        b3 = b.reshape(nchunk, HG_CHUNK, HG_DK)
        tot = b3[:, 0:1, :] if rev else b3[:, HG_CHUNK - 1:HG_CHUNK, :]
        totf = jnp.broadcast_to(tot, b3.shape).reshape(n, HG_DK)
        k = 1.0 - jnp.exp(g)
        qds.append((refs[0][0, rs, cs] * jnp.exp(b)).astype(BF16))
        kds.append((k * jnp.exp(-b)).astype(BF16))
        kcs.append((k * jnp.exp(totf - b)).astype(BF16))
        decs.append(jnp.exp(tot))

    amats = [_dot_nt(qd, kd) for qd, kd in zip(qds, kds)]
    vs = [refs[1][0, rs, cs] for refs, _, _, rs, cs in groups]
    intras = [_dot(jnp.where(tri[rev], a, 0.0).astype(BF16), v)
              for a, v, (_, rev, _, _, _) in zip(amats, vs, groups)]
    chunks = [slice(c * HG_CHUNK, (c + 1) * HG_CHUNK) for c in range(nchunk)]
    uts = [[_dot_tn(v[sl], kc[sl]) for sl in chunks] for v, kc in zip(vs, kcs)]

    chains = {}
    for gi, (_, rev, h, _, _) in enumerate(groups):
        order = range(nchunk - 1, -1, -1) if rev else range(nchunk)
        chains.setdefault((rev, h), []).extend((gi, ci) for ci in order)
    states = {key: (stb_ref if key[0] else stf_ref)[key[1]] for key in chains}
    outs = [[None] * nchunk for _ in groups]
    for t in range(nsub * nchunk):
        for key, steps in chains.items():
            gi, ci = steps[t]
            st = states[key]
            outs[gi][ci] = intras[gi][chunks[ci]] + _dot_nt(qds[gi][chunks[ci]], st.astype(BF16))
            states[key] = st * decs[gi][ci] + uts[gi][ci]
    for gi, (refs, _, _, rs, cs) in enumerate(groups):
        refs[3][0, rs, cs] = jnp.concatenate(outs[gi], axis=0)
    for (rev, h), st in states.items():
        (stb_ref if rev else stf_ref)[h] = st


def _hgrn(hq, hv, gf, gb):
    B, L, _ = hq.shape
    rows = min(HG_ROWS, L)
    nb = L // rows
    fwd = pl.BlockSpec((1, rows, HG_W), lambda b, i: (b, i, 0))
    bwd = pl.BlockSpec((1, rows, HG_W), lambda b, i: (b, nb - 1 - i, 0))
    o = jax.ShapeDtypeStruct((B, L, HG_W), F32)
    state = pltpu.VMEM((HG_HEADS, HG_DK, HG_DK), F32)
    return pl.pallas_call(
        _hgrn_kernel,
        grid=(B, nb),
        in_specs=[fwd, fwd, fwd, bwd, bwd, bwd],
        out_specs=(fwd, bwd),
        out_shape=(o, o),
        scratch_shapes=[state, state],
        compiler_params=pltpu.CompilerParams(
            dimension_semantics=("arbitrary", "arbitrary"), vmem_limit_bytes=VMEM_LIMIT),
        name="hgrn2",
    )(hq, hv, gf, hq, hv, gb)


def _attn_kernel(flag_ref, inv_ref, q_ref, k_ref, vt_ref, o_ref, acc_ref, *, tk, inner):
    tq = q_ref.shape[2]
    nk = k_ref.shape[2] // tk
    qts = [q_ref[0, j * QK_DEPTH:(j + 1) * QK_DEPTH, :] for j in range(AT_GROUP)]
    inv = inv_ref[0]

    def scores(blk, j, stabilised):
        start = pl.multiple_of(blk * tk, tk)
        st = _dot(k_ref[0, 0, pl.ds(start, tk), :], qts[j]) * inv
        return st if stabilised else jnp.exp2(st).astype(BF16)

    def run(stabilised, lookahead):
        def body(it, carry):
            pending, state = carry
            pending = list(pending)
            state = list(state)
            for n in range(inner * AT_GROUP):
                blk = it * inner + n // AT_GROUP
                j = n % AT_GROUP
                start = pl.multiple_of(blk * tk, tk)
                vb = vt_ref[0, 0, :, pl.ds(start, tk)]
                st = pending.pop(0)
                ahead = n + lookahead
                pending.append(scores(jnp.minimum(it * inner + ahead // AT_GROUP, nk - 1), ahead % AT_GROUP,
                                      stabilised))
                if stabilised:
                    m, acc = state[j]
                    m_new = jnp.maximum(m, jnp.max(st, axis=0, keepdims=True))
                    p = jnp.exp2(st - m_new).astype(BF16)
                    state[j] = (m_new, jnp.exp2(m - m_new) * acc + _dot(vb, p))
                else:
                    state[j] = state[j] + _dot(vb, st)
            return tuple(pending), tuple(state)

        zero = jnp.zeros((VT_ROWS, tq), F32)
        init = tuple((jnp.full((1, tq), -1e30, F32), zero) if stabilised else zero for _ in range(AT_GROUP))
        first = tuple(scores(min(n // AT_GROUP, nk - 1), n % AT_GROUP, stabilised) for n in range(lookahead))
        _, final = lax.fori_loop(0, nk // inner, body, (first, init))
        for j in range(AT_GROUP):
            acc_ref[j] = final[j][1] if stabilised else final[j]

    @pl.when(flag_ref[0] != 0)
    def _():
        run(False, LOOKAHEAD_BOUNDED)

    @pl.when(flag_ref[0] == 0)
    def _():
        run(True, LOOKAHEAD)

    for j in range(AT_GROUP):
        o_ref[0, j * AT_HD:(j + 1) * AT_HD, :] = (
            acc_ref[j, 0:AT_HD, :] / acc_ref[j, AT_HD:AT_HD + 1, :]).astype(o_ref.dtype)


def _attention(flag, inv, aq, ak, vt):
    B, _, L = aq.shape
    tq = min(TQ, L)
    tk = min(TK, L)
    gw = AT_GROUP * AT_HD
    return pl.pallas_call(
        functools.partial(_attn_kernel, tk=tk, inner=min(KV_BLOCKS_PER_ITER, L // tk)),
        grid=(B, AT_KV_HEADS, L // tq),
        in_specs=[
            pl.BlockSpec(memory_space=pltpu.SMEM),
            pl.BlockSpec(memory_space=pltpu.SMEM),
            pl.BlockSpec((1, AT_GROUP * QK_DEPTH, tq), lambda b, g, i: (b, g, i)),
            pl.BlockSpec((1, 1, L, QK_DEPTH), lambda b, g, i: (b, g, 0, 0)),
            pl.BlockSpec((1, 1, VT_ROWS, L), lambda b, g, i: (b, g, 0, 0)),
        ],
        out_specs=pl.BlockSpec((1, gw, tq), lambda b, g, i: (b, g, i)),
        out_shape=jax.ShapeDtypeStruct((B, AT_W, L), BF16),
        scratch_shapes=[pltpu.VMEM((AT_GROUP, VT_ROWS, tq), F32)],
        compiler_params=pltpu.CompilerParams(
            dimension_semantics=("arbitrary", "arbitrary", "arbitrary"), vmem_limit_bytes=VMEM_LIMIT),
        name="attention",
    )(flag, inv, aq, ak, vt)


def _merge_kernel(x_ref, of_ref, ob_ref, oat_ref, n1_ref, wgh_ref, wgab_ref, ong_ref, wa_ref, wb_ref,
                  wo_ref, n2_ref, *rest, ncast):
    cast_src, (h_ref, hn_ref), cast_dst = rest[:ncast], rest[ncast:ncast + 2], rest[ncast + 2:]
    _cast_blocks(cast_src, cast_dst)
    x = x_ref[0]
    u = _rms_rows(x, n1_ref[...]).astype(BF16)
    hgate = _dot(u, wgh_ref[...])
    yb = _dot_tn(oat_ref[0], wb_ref[...])
    gb = _dot(u, wgab_ref[:, D_MODEL:2 * D_MODEL])
    o = of_ref[0] + ob_ref[0]
    heads = []
    for h in range(HG_HEADS):
        sl = slice(h * HG_DK, (h + 1) * HG_DK)
        heads.append(_rms_rows(o[:, sl], ong_ref[...]))
    oa = (jnp.concatenate(heads, axis=1) * (hgate * _sigmoid(hgate))).astype(BF16)
    ga = _dot(u, wgab_ref[:, 0:D_MODEL])
    gated_b = _sigmoid(gb) * yb
    ya = _dot(oa, wa_ref[...])
    merged = (_sigmoid(ga) * ya + gated_b).astype(BF16)
    h = x + _dot(merged, wo_ref[...])
    h_ref[0] = h
    hn_ref[0] = _rms_rows(h, n2_ref[...]).astype(BF16)


def _merge(x, o_f, o_b, o_at, n1, wgh, wgab, ong, wa, wb, wo, n2, later_weights):
    B, L, D = x.shape
    tm = min(TM_MERGE, L)
    cast_specs, cast_types = _cast_plan(later_weights, B, L // tm)
    tok = lambda w: pl.BlockSpec((1, tm, w), lambda b, i: (b, i, 0))
    return pl.pallas_call(
        functools.partial(_merge_kernel, ncast=len(later_weights)),
        grid=(B, L // tm),
        in_specs=[tok(D), tok(HG_W), tok(HG_W), pl.BlockSpec((1, AT_W, tm), lambda b, i: (b, 0, i)),
                  _const_spec(n1.shape), _const_spec(wgh.shape),
                  _const_spec(wgab.shape), _const_spec(ong.shape), _const_spec(wa.shape),
                  _const_spec(wb.shape), _const_spec(wo.shape), _const_spec(n2.shape)] + cast_specs,
        out_specs=(tok(D), tok(D)) + tuple(cast_specs),
        out_shape=(jax.ShapeDtypeStruct((B, L, D), F32), jax.ShapeDtypeStruct((B, L, D), BF16))
        + tuple(cast_types),
        compiler_params=pltpu.CompilerParams(
            dimension_semantics=("arbitrary", "arbitrary"), vmem_limit_bytes=VMEM_LIMIT),
        name="merge",
    )(x, o_f, o_b, o_at, n1, wgh, wgab, ong, wa, wb, wo, n2, *later_weights)


def _ffn_kernel(h_ref, hn_ref, prev_ref, next_ref, wup_ref, cw_ref, cb_ref, wdn_ref, y_ref, ext_ref,
                act_ref):
    i = pl.program_id(1)
    nt = pl.num_programs(1)
    tm = hn_ref.shape[1]
    ext = tm + 2 * HALO
    ext_ref[0:HALO, :] = jnp.where(i > 0, prev_ref[0], jnp.zeros_like(prev_ref[0]))
    ext_ref[HALO:HALO + tm, :] = hn_ref[0]
    ext_ref[HALO + tm:ext, :] = jnp.where(i < nt - 1, next_ref[0], jnp.zeros_like(next_ref[0]))
    hx = ext_ref[...]

    def conv(t, off):
        w = cw_ref[:, off:off + FF_CHUNK]
        before = pltpu.roll(t, 1, axis=0)[HALO:HALO + tm]
        after = pltpu.roll(t, ext - 1, axis=0)[HALO:HALO + tm]
        return (before * w[0:1] + t[HALO:HALO + tm] * w[1:2] + after * w[2:3]
                + cb_ref[:, off:off + FF_CHUNK])

    def up(j):
        off = j * FF_CHUNK
        return (_dot(hx, wup_ref[:, off:off + FF_CHUNK]),
                _dot(hx, wup_ref[:, D_FF + off:D_FF + off + FF_CHUNK]))

    nchunk = D_FF // FF_CHUNK
    bounds = [round(g * nchunk / FF_DOWN_GROUPS) * FF_CHUNK for g in range(FF_DOWN_GROUPS + 1)]
    y = h_ref[0]
    nxt = up(0)
    for j in range(nchunk):
        off = j * FF_CHUNK
        upv, upg = nxt
        if j + 1 < nchunk:
            nxt = up(j + 1)
        val = conv(upv, off)
        gate = conv(upg, D_FF + off)
        act_ref[:, off:off + FF_CHUNK] = (gate * _sigmoid(gate) * val).astype(BF16)
        if off + FF_CHUNK in bounds[1:]:
            lo = bounds[bounds.index(off + FF_CHUNK) - 1]
            y = y + _dot(act_ref[:, lo:off + FF_CHUNK], wdn_ref[lo:off + FF_CHUNK, :])
    y_ref[0] = y


def _ffn(h, hn, wup, cw, cb, wdn):
    B, L, D = h.shape
    tm = min(TM_FFN, L)
    per = tm // HALO
    nh = L // HALO
    tok = pl.BlockSpec((1, tm, D), lambda b, i: (b, i, 0))
    return pl.pallas_call(
        _ffn_kernel,
        grid=(B, L // tm),
        in_specs=[
            tok, tok,
            pl.BlockSpec((1, HALO, D), lambda b, i: (b, jnp.maximum(i * per - 1, 0), 0)),
            pl.BlockSpec((1, HALO, D), lambda b, i: (b, jnp.minimum((i + 1) * per, nh - 1), 0)),
            _const_spec(wup.shape), _const_spec(cw.shape), _const_spec(cb.shape), _const_spec(wdn.shape),
        ],
        out_specs=tok,
        out_shape=jax.ShapeDtypeStruct((B, L, D), F32),
        scratch_shapes=[pltpu.VMEM((tm + 2 * HALO, D), BF16), pltpu.VMEM((tm, D_FF), BF16)],
        compiler_params=pltpu.CompilerParams(
            dimension_semantics=("arbitrary", "arbitrary"), vmem_limit_bytes=VMEM_LIMIT),
        name="convffn",
    )(h, hn, hn, hn, wup, cw, cb, wdn)


def _rope_tables(L):
    pos = np.arange(L)
    inv = ROPE_THETA ** (-np.arange(0, ROPE_AXIS_DIM, 2, dtype=np.float64) / ROPE_AXIS_DIM)
    ang_r = (pos // GRID_W)[:, None] * inv[None, :]
    ang_c = (pos % GRID_W)[:, None] * inv[None, :]
    ang = np.concatenate([ang_r, ang_r, ang_c, ang_c] * (LANES // AT_HD), axis=-1)
    return jnp.asarray(np.cos(ang), dtype=F32), jnp.asarray(np.sin(ang), dtype=F32)


def _layer(x, norm1_g, w_in, lb_f, lb_b, onorm_g, q_norm_g, k_norm_g, w_a, w_b, w_out, norm2_g,
           w_up, conv_w, conv_b, w_down):
    B, L, D = x.shape
    c_hg = 4 * HG_W
    c_gate = 5 * HG_W
    c_at = c_gate + AT_W + 2 * AT_KV_W
    whg = w_in[:, 0:c_hg].astype(BF16)
    wat = w_in[:, c_gate:c_at].astype(BF16)
    merge_weights = (w_in[:, c_hg:c_gate], w_in[:, c_at:], w_a, w_b, w_out)
    n1 = norm1_g.reshape(1, D)
    n2 = norm2_g.reshape(1, D)
    qg2 = jnp.tile(q_norm_g.reshape(1, AT_HD), (1, LANES // AT_HD))
    kg2 = jnp.tile(k_norm_g.reshape(1, AT_HD), (1, LANES // AT_HD))
    cos, sin = _rope_tables(L)
    blk = np.arange(AT_W) // AT_HD
    gsum = jnp.asarray((blk[:, None] == blk[None, :]).astype(np.float32) / AT_HD, dtype=BF16)

    q_bound = LOG2_E * jnp.max(jnp.abs(q_norm_g))
    k_bound = AT_HD ** 0.5 * jnp.max(jnp.abs(k_norm_g))
    score_bound = 1.01 * q_bound * k_bound
    flag = (score_bound <= SCORE_LIMIT).astype(jnp.int32).reshape(1)

    def f8_shift(bound):
        return jnp.clip(jnp.floor(jnp.log2(F8_TARGET_MAX / jnp.maximum(bound, 1e-30))), -60.0, 60.0)

    eq, ek = f8_shift(q_bound), f8_shift(k_bound)
    qs2 = jnp.full((1, LANES), AT_HD ** -0.5 * LOG2_E, F32) * jnp.exp2(eq)
    ks2 = jnp.full((1, LANES), 1.0, F32) * jnp.exp2(ek)
    inv = jnp.exp2(-(eq + ek)).astype(F32).reshape(1)

    hq, hv, gf, gb, aq, ak, vt, wgh, wgab, wa, wb, wo = _inproj(
        x, n1, whg, wat, lb_f, lb_b, qg2, kg2, qs2, ks2, cos, sin, gsum, merge_weights)
    o_f, o_b = _hgrn(hq, hv, gf, gb)
    o_at = _attention(flag, inv, aq, ak, vt)
    h, hn, wup, wdn = _merge(x, o_f, o_b, o_at, n1, wgh, wgab, onorm_g.reshape(1, HG_DK), wa, wb, wo, n2,
                             (w_up, w_down))
    return _ffn(h, hn, wup, conv_w, conv_b.reshape(1, 2 * D_FF), wdn)


def kernel(x, norm1_g, w_in, hg_lb_fwd, hg_lb_bwd, hg_onorm_g, q_norm_g, k_norm_g, w_branch_a,
           w_branch_b, w_out, norm2_g, w_up, conv_w, conv_b, w_down):
    depth = w_in.shape[0]
    assert depth == 1 and hg_lb_fwd.shape[0] == 2
    l = 0
    return _layer(x, norm1_g[l], w_in[l], hg_lb_fwd, hg_lb_bwd, hg_onorm_g[l], q_norm_g[l], k_norm_g[l],
                  w_branch_a[l], w_branch_b[l], w_out[l], norm2_g[l], w_up[l], conv_w[l], conv_b[l],
                  w_down[l])
```

```python
import functools

import jax
import jax.numpy as jnp
import numpy as np
from jax import lax
from jax.experimental import pallas as pl
from jax.experimental.pallas import tpu as pltpu

F32 = jnp.float32
BF16 = jnp.bfloat16
F8 = jnp.float8_e4m3fn

D_MODEL = 1024
GRID_W = 64
HG_HEADS = 4
HG_DK = 128
HG_W = HG_HEADS * HG_DK
HG_CHUNK = 64
AT_HEADS = 8
AT_KV_HEADS = 2
AT_GROUP = AT_HEADS // AT_KV_HEADS
AT_HD = 64
AT_W = AT_HEADS * AT_HD
AT_KV_W = AT_KV_HEADS * AT_HD
ROPE_THETA = 10000.0
ROPE_AXIS_DIM = AT_HD // 2
D_FF = 2816
EPS = 1e-6
LOG2_E = 1.4426950408889634

LANES = 128
BF16_SUBLANES = 16
F8_TARGET_MAX = 224.0
QK_DEPTH = 4 * AT_HD
VMEM_LIMIT = 56 * 1024 * 1024

VT_ROWS = AT_HD + BF16_SUBLANES
HALO = BF16_SUBLANES

TM_PROJ = 512
HG_SUB = 4 * HG_CHUNK
HG_ROWS = 512
TQ = 512
TK = 512
LOOKAHEAD = 2
LOOKAHEAD_BOUNDED = 2
SCORE_LIMIT = 60.0
KV_BLOCKS_PER_ITER = 16
TM_MERGE = 512
TM_FFN = 512
FF_CHUNK = 256
FF_DOWN_GROUPS = 1


def _dot(a, b):
    return jnp.dot(a, b, preferred_element_type=F32)


def _dot_nt(a, b):
    return lax.dot_general(a, b, (((1,), (1,)), ((), ())), preferred_element_type=F32)


def _dot_tn(a, b):
    return lax.dot_general(a, b, (((0,), (0,)), ((), ())), preferred_element_type=F32)


def _sigmoid(x):
    return 1.0 / (1.0 + jnp.exp(-x))


def _rms_rows(x, g):
    ms = jnp.mean(x * x, axis=-1, keepdims=True)
    return x * lax.rsqrt(ms + EPS) * g


def _const_spec(shape):
    nd = len(shape)
    return pl.BlockSpec(shape, lambda *_: (0,) * nd, pipeline_mode=pl.Buffered(1))


def _cast_plan(weights, batch, tiles):
    steps = batch * tiles
    in_specs, out_specs, types, arrays = [], [], [], []
    for w in weights:
        w, col, width = w if isinstance(w, tuple) else (w, 0, w.shape[1])
        rows = w.shape[0]
        nblk = steps
        while rows % nblk or (rows // nblk) % BF16_SUBLANES:
            nblk //= 2
        row_block = lambda b, i, nblk=nblk: (b * tiles + i) * nblk // steps
        if width == w.shape[1]:
            in_specs.append(pl.BlockSpec((rows // nblk, width), lambda b, i, f=row_block: (f(b, i), 0)))
        else:
            in_specs.append(pl.BlockSpec(
                (pl.Element(rows // nblk), pl.Element(width)),
                lambda b, i, f=row_block, col=col, r=rows // nblk: (f(b, i) * r, col)))
        out_specs.append(pl.BlockSpec((rows // nblk, width), lambda b, i, f=row_block: (f(b, i), 0)))
        types.append(jax.ShapeDtypeStruct((rows, width), BF16))
        arrays.append(w)
    return in_specs, out_specs, types, arrays


def _cast_blocks(src_refs, dst_refs):
    for src, dst in zip(src_refs, dst_refs):
        dst[...] = src[...].astype(BF16)


def _lower_bound(lb_ref):
    a0 = lb_ref[0:1, :]
    a1 = lb_ref[1:2, :]
    m = jnp.maximum(a0, a1)
    e0 = jnp.exp(a0 - m)
    e1 = jnp.exp(a1 - m)
    return e0 / (e0 + e1)


def _head_rms(t, gsum):
    sq = t * t
    hi = sq.astype(BF16)
    lo = (sq - hi.astype(F32)).astype(BF16)
    ms = _dot(hi, gsum) + _dot(lo, gsum)
    return t * lax.rsqrt(ms + EPS)


def _rope(t, cos, sin_signed, first_half):
    from_hi = pltpu.roll(t, LANES - ROPE_AXIS_DIM // 2, axis=1)
    from_lo = pltpu.roll(t, ROPE_AXIS_DIM // 2, axis=1)
    return t * cos + jnp.where(first_half, from_hi, from_lo) * sin_signed


def _split_f8(t):
    hi = t.astype(F8).astype(F32)
    return hi, t - hi


def _inproj_kernel(x_ref, n1_ref, whg_ref, wat_ref, lbf_ref, lbb_ref, qg_ref, kg_ref, qs_ref, ks_ref,
                   cos_ref, sin_ref, gsum_ref, *rest, ncast):
    cast_src, rest = rest[:ncast], rest[ncast:]
    hq_ref, hv_ref, gf_ref, gb_ref, aq_ref, ak_ref, vt_ref = rest[:7]
    _cast_blocks(cast_src, rest[7:])
    x = x_ref[0]
    tm = x.shape[0]
    u = _rms_rows(x, n1_ref[...]).astype(BF16)

    cos = cos_ref[...]
    sin = sin_ref[...]
    lane = lax.broadcasted_iota(jnp.int32, cos.shape, 1)
    first_half = (lane & (ROPE_AXIS_DIM - 1)) < (ROPE_AXIS_DIM // 2)
    sin_signed = jnp.where(first_half, -sin, sin)
    gsum = gsum_ref[...]

    def silu_q(hq):
        hq_ref[0] = hq * _sigmoid(hq)

    def values(hv):
        hv_ref[0] = hv.astype(BF16)

    def log_forget(out_ref, lb_ref):
        def epilogue(pre):
            lb = _lower_bound(lb_ref)
            out_ref[0] = jnp.log(lb + (1.0 - lb) * _sigmoid(pre))
        return epilogue

    def attn_q(aq):
        qn = _head_rms(aq, gsum)
        for c in range(AT_W // LANES):
            t = qn[:, c * LANES:(c + 1) * LANES] * qg_ref[...]
            hi, lo = _split_f8((_rope(t, cos, sin_signed, first_half) * qs_ref[...]).T)
            for hh in range(LANES // AT_HD):
                rows = slice(hh * AT_HD, (hh + 1) * AT_HD)
                head = c * (LANES // AT_HD) + hh
                aq_ref[0, head * QK_DEPTH:(head + 1) * QK_DEPTH, :] = jnp.concatenate(
                    [hi[rows], lo[rows], hi[rows], lo[rows]], axis=0).astype(F8)

    def attn_k(ak):
        kn = _head_rms(ak, gsum[0:LANES, 0:LANES]) * kg_ref[...]
        hi, lo = _split_f8(_rope(kn, cos, sin_signed, first_half) * ks_ref[...])
        for g in range(AT_KV_HEADS):
            cols = slice(g * AT_HD, (g + 1) * AT_HD)
            ak_ref[0, g] = jnp.concatenate([hi[:, cols], hi[:, cols], lo[:, cols], lo[:, cols]],
                                           axis=1).astype(F8)

    def attn_v(av):
        avt = av.T
        row = lax.broadcasted_iota(jnp.int32, (VT_ROWS - AT_HD, tm), 0)
        aug = jnp.where(row == 0, 1.0, 0.0).astype(BF16)
        for g in range(AT_KV_HEADS):
            vt_ref[0, g, 0:AT_HD, :] = avt[g * AT_HD:(g + 1) * AT_HD].astype(BF16)
            vt_ref[0, g, AT_HD:VT_ROWS, :] = aug

    sections = (
        (wat_ref, 0, AT_W, attn_q),
        (wat_ref, AT_W, AT_KV_W, attn_k),
        (wat_ref, AT_W + AT_KV_W, AT_KV_W, attn_v),
        (whg_ref, 2 * HG_W, HG_W, log_forget(gf_ref, lbf_ref)),
        (whg_ref, 3 * HG_W, HG_W, log_forget(gb_ref, lbb_ref)),
        (whg_ref, 0, HG_W, silu_q),
        (whg_ref, HG_W, HG_W, values),
    )

    def project(s):
        w_ref, off, width, _ = sections[s]
        return _dot(u, w_ref[:, off:off + width])

    nxt = project(0)
    for s in range(len(sections)):
        cur = nxt
        if s + 1 < len(sections):
            nxt = project(s + 1)
        sections[s][3](cur)


def _inproj(x, n1, whg, wat, lbf, lbb, qg2, kg2, qs2, ks2, cos, sin, gsum, later_weights):
    B, L, D = x.shape
    tm = min(TM_PROJ, L)
    nt = L // tm
    cast_in, cast_out, cast_types, cast_arrays = _cast_plan(later_weights, B, nt)
    tok = lambda w: pl.BlockSpec((1, tm, w), lambda b, i: (b, i, 0))
    out_shape = (
        jax.ShapeDtypeStruct((B, L, HG_W), F32),
        jax.ShapeDtypeStruct((B, L, HG_W), BF16),
        jax.ShapeDtypeStruct((B, L, HG_W), F32),
        jax.ShapeDtypeStruct((B, L, HG_W), F32),
        jax.ShapeDtypeStruct((B, AT_HEADS * QK_DEPTH, L), F8),
        jax.ShapeDtypeStruct((B, AT_KV_HEADS, L, QK_DEPTH), F8),
        jax.ShapeDtypeStruct((B, AT_KV_HEADS, VT_ROWS, L), BF16),
    )
    out_specs = (
        tok(HG_W), tok(HG_W), tok(HG_W), tok(HG_W),
        pl.BlockSpec((1, AT_HEADS * QK_DEPTH, tm), lambda b, i: (b, 0, i)),
        pl.BlockSpec((1, AT_KV_HEADS, tm, QK_DEPTH), lambda b, i: (b, 0, i, 0)),
        pl.BlockSpec((1, AT_KV_HEADS, VT_ROWS, tm), lambda b, i: (b, 0, 0, i)),
    )
    in_specs = [
        tok(D), _const_spec(n1.shape), _const_spec(whg.shape), _const_spec(wat.shape),
        _const_spec(lbf.shape), _const_spec(lbb.shape), _const_spec(qg2.shape), _const_spec(kg2.shape),
        _const_spec(qs2.shape), _const_spec(ks2.shape),
        pl.BlockSpec((tm, LANES), lambda b, i: (i, 0)),
        pl.BlockSpec((tm, LANES), lambda b, i: (i, 0)),
        _const_spec(gsum.shape),
    ]
    return pl.pallas_call(
        functools.partial(_inproj_kernel, ncast=len(later_weights)),
        grid=(B, nt),
        in_specs=in_specs + cast_in,
        out_specs=out_specs + tuple(cast_out),
        out_shape=out_shape + tuple(cast_types),
        compiler_params=pltpu.CompilerParams(
            dimension_semantics=("arbitrary", "arbitrary"), vmem_limit_bytes=VMEM_LIMIT),
        name="inproj",
    )(x, n1, whg, wat, lbf, lbb, qg2, kg2, qs2, ks2, cos, sin, gsum, *cast_arrays)


def _chunk_mask(n, reverse):
    shift = HG_CHUNK.bit_length() - 1
    r = lax.broadcasted_iota(jnp.int32, (n, n), 0)
    c = lax.broadcasted_iota(jnp.int32, (n, n), 1)
    same = lax.shift_right_logical(r, shift) == lax.shift_right_logical(c, shift)
    return same & ((c >= r) if reverse else (c <= r))


def _hgrn_kernel(qf_ref, vf_ref, gf_ref, qb_ref, vb_ref, gb_ref, of_ref, ob_ref, stf_ref, stb_ref):
    @pl.when(pl.program_id(1) == 0)
    def _():
        stf_ref[...] = jnp.zeros_like(stf_ref)
        stb_ref[...] = jnp.zeros_like(stb_ref)

    n = HG_SUB
    nsub = qf_ref.shape[1] // n
    nchunk = n // HG_CHUNK
    tri = {rev: _chunk_mask(n, rev) for rev in (False, True)}

    groups = []
    for h in range(HG_HEADS):
        cols = slice(h * HG_DK, (h + 1) * HG_DK)
        for s in range(nsub):
            groups.append(((qf_ref, vf_ref, gf_ref, of_ref), False, h, slice(s * n, (s + 1) * n), cols))
        for s in range(nsub - 1, -1, -1):
            groups.append(((qb_ref, vb_ref, gb_ref, ob_ref), True, h, slice(s * n, (s + 1) * n), cols))

    gs = [refs[2][0, rs, cs] for refs, _, _, rs, cs in groups]
    pos = lax.broadcasted_iota(jnp.int32, (n, HG_DK), 0) & (HG_CHUNK - 1)
    bs = []
    for g, (_, rev, _, _, _) in zip(gs, groups):
        b = g
        step = 1
        while step < HG_CHUNK:
            if rev:
                b = b + jnp.where(pos < HG_CHUNK - step, pltpu.roll(b, n - step, axis=0), 0.0)
            else:
                b = b + jnp.where(pos >= step, pltpu.roll(b, step, axis=0), 0.0)
            step *= 2
        bs.append(b)

    qds, kds, kcs, decs = [], [], [], []
    for g, b, (refs, rev, _, rs, cs) in zip(gs, bs, groups):
        b3 = b.reshape(nchunk, HG_CHUNK, HG_DK)
        tot = b3[:, 0:1, :] if rev else b3[:, HG_CHUNK - 1:HG_CHUNK, :]
        totf = jnp.broadcast_to(tot, b3.shape).reshape(n, HG_DK)
        k = 1.0 - jnp.exp(g)
        qds.append((refs[0][0, rs, cs] * jnp.exp(b)).astype(BF16))
        kds.append((k * jnp.exp(-b)).astype(BF16))
        kcs.append((k * jnp.exp(totf - b)).astype(BF16))
        decs.append(jnp.exp(tot))

    amats = [_dot_nt(qd, kd) for qd, kd in zip(qds, kds)]
    vs = [refs[1][0, rs, cs] for refs, _, _, rs, cs in groups]
    intras = [_dot(jnp.where(tri[rev], a, 0.0).astype(BF16), v)
              for a, v, (_, rev, _, _, _) in zip(amats, vs, groups)]
    chunks = [slice(c * HG_CHUNK, (c + 1) * HG_CHUNK) for c in range(nchunk)]
    uts = [[_dot_tn(v[sl], kc[sl]) for sl in chunks] for v, kc in zip(vs, kcs)]

    chains = {}
    for gi, (_, rev, h, _, _) in enumerate(groups):
        order = range(nchunk - 1, -1, -1) if rev else range(nchunk)
        chains.setdefault((rev, h), []).extend((gi, ci) for ci in order)
    states = {key: (stb_ref if key[0] else stf_ref)[key[1]] for key in chains}
    outs = [[None] * nchunk for _ in groups]
    for t in range(nsub * nchunk):
        for key, steps in chains.items():
            gi, ci = steps[t]
            st = states[key]
            outs[gi][ci] = intras[gi][chunks[ci]] + _dot_nt(qds[gi][chunks[ci]], st.astype(BF16))
            states[key] = st * decs[gi][ci] + uts[gi][ci]
    for gi, (refs, _, _, rs, cs) in enumerate(groups):
        refs[3][0, rs, cs] = jnp.concatenate(outs[gi], axis=0)
    for (rev, h), st in states.items():
        (stb_ref if rev else stf_ref)[h] = st


def _hgrn(hq, hv, gf, gb):
    B, L, _ = hq.shape
    rows = min(HG_ROWS, L)
    nb = L // rows
    fwd = pl.BlockSpec((1, rows, HG_W), lambda b, i: (b, i, 0))
    bwd = pl.BlockSpec((1, rows, HG_W), lambda b, i: (b, nb - 1 - i, 0))
    o = jax.ShapeDtypeStruct((B, L, HG_W), F32)
    state = pltpu.VMEM((HG_HEADS, HG_DK, HG_DK), F32)
    return pl.pallas_call(
        _hgrn_kernel,
        grid=(B, nb),
        in_specs=[fwd, fwd, fwd, bwd, bwd, bwd],
        out_specs=(fwd, bwd),
        out_shape=(o, o),
        scratch_shapes=[state, state],
        compiler_params=pltpu.CompilerParams(
            dimension_semantics=("arbitrary", "arbitrary"), vmem_limit_bytes=VMEM_LIMIT),
        name="hgrn2",
    )(hq, hv, gf, hq, hv, gb)


def _attn_kernel(flag_ref, inv_ref, q_ref, k_ref, vt_ref, o_ref, acc_ref, *, tk, inner):
    tq = q_ref.shape[2]
    nk = k_ref.shape[2] // tk
    qts = [q_ref[0, j * QK_DEPTH:(j + 1) * QK_DEPTH, :] for j in range(AT_GROUP)]
    inv = inv_ref[0]

    def scores(blk, j, stabilised):
        start = pl.multiple_of(blk * tk, tk)
        st = _dot(k_ref[0, 0, pl.ds(start, tk), :], qts[j]) * inv
        return st if stabilised else jnp.exp2(st).astype(BF16)

    def run(stabilised, lookahead):
        def body(it, carry):
            pending, state = carry
            pending = list(pending)
            state = list(state)
            for n in range(inner * AT_GROUP):
                blk = it * inner + n // AT_GROUP
                j = n % AT_GROUP
                start = pl.multiple_of(blk * tk, tk)
                vb = vt_ref[0, 0, :, pl.ds(start, tk)]
                st = pending.pop(0)
                ahead = n + lookahead
                pending.append(scores(jnp.minimum(it * inner + ahead // AT_GROUP, nk - 1), ahead % AT_GROUP,
                                      stabilised))
                if stabilised:
                    m, acc = state[j]
                    m_new = jnp.maximum(m, jnp.max(st, axis=0, keepdims=True))
                    p = jnp.exp2(st - m_new).astype(BF16)
                    state[j] = (m_new, jnp.exp2(m - m_new) * acc + _dot(vb, p))
                else:
                    state[j] = state[j] + _dot(vb, st)
            return tuple(pending), tuple(state)

        zero = jnp.zeros((VT_ROWS, tq), F32)
        init = tuple((jnp.full((1, tq), -1e30, F32), zero) if stabilised else zero for _ in range(AT_GROUP))
        first = tuple(scores(min(n // AT_GROUP, nk - 1), n % AT_GROUP, stabilised) for n in range(lookahead))
        _, final = lax.fori_loop(0, nk // inner, body, (first, init))
        for j in range(AT_GROUP):
            acc_ref[j] = final[j][1] if stabilised else final[j]

    @pl.when(flag_ref[0] != 0)
    def _():
        run(False, LOOKAHEAD_BOUNDED)

    @pl.when(flag_ref[0] == 0)
    def _():
        run(True, LOOKAHEAD)

    for j in range(AT_GROUP):
        o_ref[0, j * AT_HD:(j + 1) * AT_HD, :] = (
            acc_ref[j, 0:AT_HD, :] / acc_ref[j, AT_HD:AT_HD + 1, :]).astype(o_ref.dtype)


def _attention(flag, inv, aq, ak, vt):
    B, _, L = aq.shape
    tq = min(TQ, L)
    tk = min(TK, L)
    gw = AT_GROUP * AT_HD
    return pl.pallas_call(
        functools.partial(_attn_kernel, tk=tk, inner=min(KV_BLOCKS_PER_ITER, L // tk)),
        grid=(B, AT_KV_HEADS, L // tq),
        in_specs=[
            pl.BlockSpec(memory_space=pltpu.SMEM),
            pl.BlockSpec(memory_space=pltpu.SMEM),
            pl.BlockSpec((1, AT_GROUP * QK_DEPTH, tq), lambda b, g, i: (b, g, i)),
            pl.BlockSpec((1, 1, L, QK_DEPTH), lambda b, g, i: (b, g, 0, 0)),
            pl.BlockSpec((1, 1, VT_ROWS, L), lambda b, g, i: (b, g, 0, 0)),
        ],
        out_specs=pl.BlockSpec((1, gw, tq), lambda b, g, i: (b, g, i)),
        out_shape=jax.ShapeDtypeStruct((B, AT_W, L), BF16),
        scratch_shapes=[pltpu.VMEM((AT_GROUP, VT_ROWS, tq), F32)],
        compiler_params=pltpu.CompilerParams(
            dimension_semantics=("arbitrary", "arbitrary", "arbitrary"), vmem_limit_bytes=VMEM_LIMIT),
        name="attention",
    )(flag, inv, aq, ak, vt)


def _merge_kernel(x_ref, of_ref, ob_ref, oat_ref, n1_ref, wgh_ref, wgab_ref, ong_ref, wa_ref, wb_ref,
                  wo_ref, n2_ref, *rest, ncast):
    cast_src, (h_ref, hn_ref), cast_dst = rest[:ncast], rest[ncast:ncast + 2], rest[ncast + 2:]
    _cast_blocks(cast_src, cast_dst)
    x = x_ref[0]
    u = _rms_rows(x, n1_ref[...]).astype(BF16)
    hgate = _dot(u, wgh_ref[...])
    yb = _dot_tn(oat_ref[0], wb_ref[...])
    gb = _dot(u, wgab_ref[:, D_MODEL:2 * D_MODEL])
    o = of_ref[0] + ob_ref[0]
    heads = []
    for h in range(HG_HEADS):
        sl = slice(h * HG_DK, (h + 1) * HG_DK)
        heads.append(_rms_rows(o[:, sl], ong_ref[...]))
    oa = (jnp.concatenate(heads, axis=1) * (hgate * _sigmoid(hgate))).astype(BF16)
    ga = _dot(u, wgab_ref[:, 0:D_MODEL])
    gated_b = _sigmoid(gb) * yb
    ya = _dot(oa, wa_ref[...])
    merged = (_sigmoid(ga) * ya + gated_b).astype(BF16)
    h = x + _dot(merged, wo_ref[...])
    h_ref[0] = h
    hn_ref[0] = _rms_rows(h, n2_ref[...]).astype(BF16)


def _merge(x, o_f, o_b, o_at, n1, wgh, wgab, ong, wa, wb, wo, n2, later_weights):
    B, L, D = x.shape
    tm = min(TM_MERGE, L)
    cast_in, cast_out, cast_types, cast_arrays = _cast_plan(later_weights, B, L // tm)
    tok = lambda w: pl.BlockSpec((1, tm, w), lambda b, i: (b, i, 0))
    return pl.pallas_call(
        functools.partial(_merge_kernel, ncast=len(later_weights)),
        grid=(B, L // tm),
        in_specs=[tok(D), tok(HG_W), tok(HG_W), pl.BlockSpec((1, AT_W, tm), lambda b, i: (b, 0, i)),
                  _const_spec(n1.shape), _const_spec(wgh.shape),
                  _const_spec(wgab.shape), _const_spec(ong.shape), _const_spec(wa.shape),
                  _const_spec(wb.shape), _const_spec(wo.shape), _const_spec(n2.shape)] + cast_in,
        out_specs=(tok(D), tok(D)) + tuple(cast_out),
        out_shape=(jax.ShapeDtypeStruct((B, L, D), F32), jax.ShapeDtypeStruct((B, L, D), BF16))
        + tuple(cast_types),
        compiler_params=pltpu.CompilerParams(
            dimension_semantics=("arbitrary", "arbitrary"), vmem_limit_bytes=VMEM_LIMIT),
        name="merge",
    )(x, o_f, o_b, o_at, n1, wgh, wgab, ong, wa, wb, wo, n2, *cast_arrays)


def _ffn_kernel(h_ref, hn_ref, prev_ref, next_ref, wup_ref, cw_ref, cb_ref, wdn_ref, y_ref, ext_ref,
                act_ref):
    i = pl.program_id(1)
    nt = pl.num_programs(1)
    tm = hn_ref.shape[1]
    ext = tm + 2 * HALO
    ext_ref[0:HALO, :] = jnp.where(i > 0, prev_ref[0], jnp.zeros_like(prev_ref[0]))
    ext_ref[HALO:HALO + tm, :] = hn_ref[0]
    ext_ref[HALO + tm:ext, :] = jnp.where(i < nt - 1, next_ref[0], jnp.zeros_like(next_ref[0]))
    hx = ext_ref[...]

    def conv(t, off):
        w = cw_ref[:, off:off + FF_CHUNK]
        before = pltpu.roll(t, 1, axis=0)[HALO:HALO + tm]
        after = pltpu.roll(t, ext - 1, axis=0)[HALO:HALO + tm]
        return (before * w[0:1] + t[HALO:HALO + tm] * w[1:2] + after * w[2:3]
                + cb_ref[:, off:off + FF_CHUNK])

    def up(j):
        off = j * FF_CHUNK
        return (_dot(hx, wup_ref[:, off:off + FF_CHUNK]),
                _dot(hx, wup_ref[:, D_FF + off:D_FF + off + FF_CHUNK]))

    nchunk = D_FF // FF_CHUNK
    bounds = [round(g * nchunk / FF_DOWN_GROUPS) * FF_CHUNK for g in range(FF_DOWN_GROUPS + 1)]
    y = h_ref[0]
    nxt = up(0)
    for j in range(nchunk):
        off = j * FF_CHUNK
        upv, upg = nxt
        if j + 1 < nchunk:
            nxt = up(j + 1)
        val = conv(upv, off)
        gate = conv(upg, D_FF + off)
        act_ref[:, off:off + FF_CHUNK] = (gate * _sigmoid(gate) * val).astype(BF16)
        if off + FF_CHUNK in bounds[1:]:
            lo = bounds[bounds.index(off + FF_CHUNK) - 1]
            y = y + _dot(act_ref[:, lo:off + FF_CHUNK], wdn_ref[lo:off + FF_CHUNK, :])
    y_ref[0] = y


def _ffn(h, hn, wup, cw, cb, wdn):
    B, L, D = h.shape
    tm = min(TM_FFN, L)
    per = tm // HALO
    nh = L // HALO
    tok = pl.BlockSpec((1, tm, D), lambda b, i: (b, i, 0))
    return pl.pallas_call(
        _ffn_kernel,
        grid=(B, L // tm),
        in_specs=[
            tok, tok,
            pl.BlockSpec((1, HALO, D), lambda b, i: (b, jnp.maximum(i * per - 1, 0), 0)),
            pl.BlockSpec((1, HALO, D), lambda b, i: (b, jnp.minimum((i + 1) * per, nh - 1), 0)),
            _const_spec(wup.shape), _const_spec(cw.shape), _const_spec(cb.shape), _const_spec(wdn.shape),
        ],
        out_specs=tok,
        out_shape=jax.ShapeDtypeStruct((B, L, D), F32),
        scratch_shapes=[pltpu.VMEM((tm + 2 * HALO, D), BF16), pltpu.VMEM((tm, D_FF), BF16)],
        compiler_params=pltpu.CompilerParams(
            dimension_semantics=("arbitrary", "arbitrary"), vmem_limit_bytes=VMEM_LIMIT),
        name="convffn",
    )(h, hn, hn, hn, wup, cw, cb, wdn)


def _rope_tables(L):
    pos = np.arange(L)
    inv = ROPE_THETA ** (-np.arange(0, ROPE_AXIS_DIM, 2, dtype=np.float64) / ROPE_AXIS_DIM)
    ang_r = (pos // GRID_W)[:, None] * inv[None, :]
    ang_c = (pos % GRID_W)[:, None] * inv[None, :]
    ang = np.concatenate([ang_r, ang_r, ang_c, ang_c] * (LANES // AT_HD), axis=-1)
    return jnp.asarray(np.cos(ang), dtype=F32), jnp.asarray(np.sin(ang), dtype=F32)


def _layer(x, norm1_g, w_in, lb_f, lb_b, onorm_g, q_norm_g, k_norm_g, w_a, w_b, w_out, norm2_g,
           w_up, conv_w, conv_b, w_down):
    B, L, D = x.shape
    c_hg = 4 * HG_W
    c_gate = 5 * HG_W
    c_at = c_gate + AT_W + 2 * AT_KV_W
    whg = w_in[:, 0:c_hg].astype(BF16)
    wat = w_in[:, c_gate:c_at].astype(BF16)
    merge_weights = ((w_in, c_hg, c_gate - c_hg), (w_in, c_at, w_in.shape[1] - c_at),
                     w_a, w_b, w_out)
    n1 = norm1_g.reshape(1, D)
    n2 = norm2_g.reshape(1, D)
    qg2 = jnp.tile(q_norm_g.reshape(1, AT_HD), (1, LANES // AT_HD))
    kg2 = jnp.tile(k_norm_g.reshape(1, AT_HD), (1, LANES // AT_HD))
    cos, sin = _rope_tables(L)
    blk = np.arange(AT_W) // AT_HD
    gsum = jnp.asarray((blk[:, None] == blk[None, :]).astype(np.float32) / AT_HD, dtype=BF16)

    q_bound = LOG2_E * jnp.max(jnp.abs(q_norm_g))
    k_bound = AT_HD ** 0.5 * jnp.max(jnp.abs(k_norm_g))
    score_bound = 1.01 * q_bound * k_bound
    flag = (score_bound <= SCORE_LIMIT).astype(jnp.int32).reshape(1)

    def f8_shift(bound):
        return jnp.clip(jnp.floor(jnp.log2(F8_TARGET_MAX / jnp.maximum(bound, 1e-30))), -60.0, 60.0)

    eq, ek = f8_shift(q_bound), f8_shift(k_bound)
    qs2 = jnp.full((1, LANES), AT_HD ** -0.5 * LOG2_E, F32) * jnp.exp2(eq)
    ks2 = jnp.full((1, LANES), 1.0, F32) * jnp.exp2(ek)
    inv = jnp.exp2(-(eq + ek)).astype(F32).reshape(1)

    hq, hv, gf, gb, aq, ak, vt, wgh, wgab, wa, wb, wo = _inproj(
        x, n1, whg, wat, lb_f, lb_b, qg2, kg2, qs2, ks2, cos, sin, gsum, merge_weights)
    o_f, o_b = _hgrn(hq, hv, gf, gb)
    o_at = _attention(flag, inv, aq, ak, vt)
    h, hn, wup, wdn = _merge(x, o_f, o_b, o_at, n1, wgh, wgab, onorm_g.reshape(1, HG_DK), wa, wb, wo, n2,
                             (w_up, w_down))
    return _ffn(h, hn, wup, conv_w, conv_b.reshape(1, 2 * D_FF), wdn)


def kernel(x, norm1_g, w_in, hg_lb_fwd, hg_lb_bwd, hg_onorm_g, q_norm_g, k_norm_g, w_branch_a,
           w_branch_b, w_out, norm2_g, w_up, conv_w, conv_b, w_down):
    depth = w_in.shape[0]
    assert depth == 1 and hg_lb_fwd.shape[0] == 2
    l = 0
    return _layer(x, norm1_g[l], w_in[l], hg_lb_fwd, hg_lb_bwd, hg_onorm_g[l], q_norm_g[l], k_norm_g[l],
                  w_branch_a[l], w_branch_b[l], w_out[l], norm2_g[l], w_up[l], conv_w[l], conv_b[l],
                  w_down[l])
```

```python
import functools

import jax
import jax.numpy as jnp
import numpy as np
from jax import lax
from jax.experimental import pallas as pl
from jax.experimental.pallas import tpu as pltpu

F32 = jnp.float32
BF16 = jnp.bfloat16
F8 = jnp.float8_e4m3fn

D_MODEL = 1024
GRID_W = 64
HG_HEADS = 4
HG_DK = 128
HG_W = HG_HEADS * HG_DK
HG_CHUNK = 64
AT_HEADS = 8
AT_KV_HEADS = 2
AT_GROUP = AT_HEADS // AT_KV_HEADS
AT_HD = 64
AT_W = AT_HEADS * AT_HD
AT_KV_W = AT_KV_HEADS * AT_HD
ROPE_THETA = 10000.0
ROPE_AXIS_DIM = AT_HD // 2
D_FF = 2816
EPS = 1e-6
LOG2_E = 1.4426950408889634

LANES = 128
BF16_SUBLANES = 16
F8_TARGET_MAX = 224.0
QK_DEPTH = 4 * AT_HD
VMEM_LIMIT = 56 * 1024 * 1024

VT_ROWS = AT_HD + BF16_SUBLANES
HALO = BF16_SUBLANES

TM_PROJ = 512
HG_SUB = 4 * HG_CHUNK
HG_ROWS = 1024
TQ = 512
TK = 512
LOOKAHEAD = 2
LOOKAHEAD_BOUNDED = 2
SCORE_LIMIT = 60.0
KV_BLOCKS_PER_ITER = 16
TM_MERGE = 512
MERGE_SPLIT = 2
TM_FFN = 512
FF_CHUNK = 256
FF_DOWN_GROUPS = 1


def _dot(a, b):
    return jnp.dot(a, b, preferred_element_type=F32)


def _dot_nt(a, b):
    return lax.dot_general(a, b, (((1,), (1,)), ((), ())), preferred_element_type=F32)


def _dot_tn(a, b):
    return lax.dot_general(a, b, (((0,), (0,)), ((), ())), preferred_element_type=F32)


def _sigmoid(x):
    return 1.0 / (1.0 + jnp.exp(-x))


def _rms_rows(x, g):
    ms = jnp.mean(x * x, axis=-1, keepdims=True)
    return x * lax.rsqrt(ms + EPS) * g


def _const_spec(shape):
    nd = len(shape)
    return pl.BlockSpec(shape, lambda *_: (0,) * nd, pipeline_mode=pl.Buffered(1))


def _cast_plan(weights, batch, tiles):
    steps = batch * tiles
    in_specs, out_specs, types, arrays = [], [], [], []
    for w in weights:
        w, col, width = w if isinstance(w, tuple) else (w, 0, w.shape[1])
        rows = w.shape[0]
        nblk = steps
        while rows % nblk or (rows // nblk) % BF16_SUBLANES:
            nblk //= 2
        row_block = lambda b, i, nblk=nblk: (b * tiles + i) * nblk // steps
        if width == w.shape[1]:
            in_specs.append(pl.BlockSpec((rows // nblk, width), lambda b, i, f=row_block: (f(b, i), 0)))
        else:
            in_specs.append(pl.BlockSpec(
                (pl.Element(rows // nblk), pl.Element(width)),
                lambda b, i, f=row_block, col=col, r=rows // nblk: (f(b, i) * r, col)))
        out_specs.append(pl.BlockSpec((rows // nblk, width), lambda b, i, f=row_block: (f(b, i), 0)))
        types.append(jax.ShapeDtypeStruct((rows, width), BF16))
        arrays.append(w)
    return in_specs, out_specs, types, arrays


def _cast_blocks(src_refs, dst_refs):
    for src, dst in zip(src_refs, dst_refs):
        dst[...] = src[...].astype(BF16)


def _lower_bound(lb_ref):
    a0 = lb_ref[0:1, :]
    a1 = lb_ref[1:2, :]
    m = jnp.maximum(a0, a1)
    e0 = jnp.exp(a0 - m)
    e1 = jnp.exp(a1 - m)
    return e0 / (e0 + e1)


def _head_rms(t, gsum):
    sq = t * t
    hi = sq.astype(BF16)
    lo = (sq - hi.astype(F32)).astype(BF16)
    ms = _dot(hi, gsum) + _dot(lo, gsum)
    return t * lax.rsqrt(ms + EPS)


def _rope(t, cos, sin_signed, first_half):
    from_hi = pltpu.roll(t, LANES - ROPE_AXIS_DIM // 2, axis=1)
    from_lo = pltpu.roll(t, ROPE_AXIS_DIM // 2, axis=1)
    return t * cos + jnp.where(first_half, from_hi, from_lo) * sin_signed


def _split_f8(t):
    hi = t.astype(F8).astype(F32)
    return hi, t - hi


def _inproj_kernel(x_ref, n1_ref, whg32_ref, wat32_ref, lbf_ref, lbb_ref, qg_ref, kg_ref, qs_ref, ks_ref,
                   cos_ref, sin_ref, gsum_ref, *rest, ncast):
    cast_src, rest = rest[:ncast], rest[ncast:]
    hq_ref, hv_ref, gf_ref, gb_ref, aq_ref, ak_ref, vt_ref = rest[:7]
    cast_dst, (whg_ref, wat_ref) = rest[7:7 + ncast], rest[7 + ncast:]
    _cast_blocks(cast_src, cast_dst)

    @pl.when((pl.program_id(0) == 0) & (pl.program_id(1) == 0))
    def _():
        _cast_blocks((whg32_ref, wat32_ref), (whg_ref, wat_ref))

    x = x_ref[0]
    tm = x.shape[0]
    u = _rms_rows(x, n1_ref[...]).astype(BF16)

    cos = cos_ref[...]
    sin = sin_ref[...]
    lane = lax.broadcasted_iota(jnp.int32, cos.shape, 1)
    first_half = (lane & (ROPE_AXIS_DIM - 1)) < (ROPE_AXIS_DIM // 2)
    sin_signed = jnp.where(first_half, -sin, sin)
    gsum = gsum_ref[...]

    def silu_q(hq):
        hq_ref[0] = hq * _sigmoid(hq)

    def values(hv):
        hv_ref[0] = hv.astype(BF16)

    def log_forget(out_ref, lb_ref):
        def epilogue(pre):
            lb = _lower_bound(lb_ref)
            out_ref[0] = jnp.log(lb + (1.0 - lb) * _sigmoid(pre))
        return epilogue

    def attn_q(aq):
        qn = _head_rms(aq, gsum)
        for c in range(AT_W // LANES):
            t = qn[:, c * LANES:(c + 1) * LANES] * qg_ref[...]
            hi, lo = _split_f8((_rope(t, cos, sin_signed, first_half) * qs_ref[...]).T)
            for hh in range(LANES // AT_HD):
                rows = slice(hh * AT_HD, (hh + 1) * AT_HD)
                head = c * (LANES // AT_HD) + hh
                aq_ref[0, head * QK_DEPTH:(head + 1) * QK_DEPTH, :] = jnp.concatenate(
                    [hi[rows], lo[rows], hi[rows], lo[rows]], axis=0).astype(F8)

    def attn_k(ak):
        kn = _head_rms(ak, gsum[0:LANES, 0:LANES]) * kg_ref[...]
        hi, lo = _split_f8(_rope(kn, cos, sin_signed, first_half) * ks_ref[...])
        for g in range(AT_KV_HEADS):
            cols = slice(g * AT_HD, (g + 1) * AT_HD)
            ak_ref[0, g] = jnp.concatenate([hi[:, cols], hi[:, cols], lo[:, cols], lo[:, cols]],
                                           axis=1).astype(F8)

    def attn_v(av):
        avt = av.T
        row = lax.broadcasted_iota(jnp.int32, (VT_ROWS - AT_HD, tm), 0)
        aug = jnp.where(row == 0, 1.0, 0.0).astype(BF16)
        for g in range(AT_KV_HEADS):
            vt_ref[0, g, 0:AT_HD, :] = avt[g * AT_HD:(g + 1) * AT_HD].astype(BF16)
            vt_ref[0, g, AT_HD:VT_ROWS, :] = aug

    sections = (
        (wat_ref, 0, AT_W, attn_q),
        (wat_ref, AT_W, AT_KV_W, attn_k),
        (wat_ref, AT_W + AT_KV_W, AT_KV_W, attn_v),
        (whg_ref, 2 * HG_W, HG_W, log_forget(gf_ref, lbf_ref)),
        (whg_ref, 3 * HG_W, HG_W, log_forget(gb_ref, lbb_ref)),
        (whg_ref, 0, HG_W, silu_q),
        (whg_ref, HG_W, HG_W, values),
    )

    def project(s):
        w_ref, off, width, _ = sections[s]
        return _dot(u, w_ref[:, off:off + width])

    nxt = project(0)
    for s in range(len(sections)):
        cur = nxt
        if s + 1 < len(sections):
            nxt = project(s + 1)
        sections[s][3](cur)


def _inproj(x, n1, w_in, hg_cols, at_cols, lbf, lbb, qg2, kg2, qs2, ks2, cos, sin, gsum, later_weights):
    B, L, D = x.shape
    tm = min(TM_PROJ, L)
    nt = L // tm

    def col_range(cols):
        first, width = cols
        return pl.BlockSpec((pl.Element(D), pl.Element(width)), lambda b, i: (0, first),
                            pipeline_mode=pl.Buffered(1))

    cast_in, cast_out, cast_types, cast_arrays = _cast_plan(later_weights, B, nt)
    tok = lambda w: pl.BlockSpec((1, tm, w), lambda b, i: (b, i, 0))
    out_shape = (
        jax.ShapeDtypeStruct((B, L, HG_W), F32),
        jax.ShapeDtypeStruct((B, L, HG_W), BF16),
        jax.ShapeDtypeStruct((B, L, HG_W), F32),
        jax.ShapeDtypeStruct((B, L, HG_W), F32),
        jax.ShapeDtypeStruct((B, AT_HEADS * QK_DEPTH, L), F8),
        jax.ShapeDtypeStruct((B, AT_KV_HEADS, L, QK_DEPTH), F8),
        jax.ShapeDtypeStruct((B, AT_KV_HEADS, VT_ROWS, L), BF16),
    )
    out_specs = (
        tok(HG_W), tok(HG_W), tok(HG_W), tok(HG_W),
        pl.BlockSpec((1, AT_HEADS * QK_DEPTH, tm), lambda b, i: (b, 0, i)),
        pl.BlockSpec((1, AT_KV_HEADS, tm, QK_DEPTH), lambda b, i: (b, 0, i, 0)),
        pl.BlockSpec((1, AT_KV_HEADS, VT_ROWS, tm), lambda b, i: (b, 0, 0, i)),
    )
    in_specs = [
        tok(D), _const_spec(n1.shape), col_range(hg_cols), col_range(at_cols),
        _const_spec(lbf.shape), _const_spec(lbb.shape), _const_spec(qg2.shape), _const_spec(kg2.shape),
        _const_spec(qs2.shape), _const_spec(ks2.shape),
        pl.BlockSpec((tm, LANES), lambda b, i: (i, 0)),
        pl.BlockSpec((tm, LANES), lambda b, i: (i, 0)),
        _const_spec(gsum.shape),
    ]
    return pl.pallas_call(
        functools.partial(_inproj_kernel, ncast=len(later_weights)),
        grid=(B, nt),
        in_specs=in_specs + cast_in,
        out_specs=out_specs + tuple(cast_out),
        out_shape=out_shape + tuple(cast_types),
        scratch_shapes=[pltpu.VMEM((D, hg_cols[1]), BF16), pltpu.VMEM((D, at_cols[1]), BF16)],
        compiler_params=pltpu.CompilerParams(
            dimension_semantics=("arbitrary", "arbitrary"), vmem_limit_bytes=VMEM_LIMIT),
        name="inproj",
    )(x, n1, w_in, w_in, lbf, lbb, qg2, kg2, qs2, ks2, cos, sin, gsum, *cast_arrays)


def _chunk_mask(n, reverse):
    shift = HG_CHUNK.bit_length() - 1
    r = lax.broadcasted_iota(jnp.int32, (n, n), 0)
    c = lax.broadcasted_iota(jnp.int32, (n, n), 1)
    same = lax.shift_right_logical(r, shift) == lax.shift_right_logical(c, shift)
    return same & ((c >= r) if reverse else (c <= r))


def _hgrn_kernel(qf_ref, vf_ref, gf_ref, qb_ref, vb_ref, gb_ref, of_ref, ob_ref, stf_ref, stb_ref):
    @pl.when(pl.program_id(1) == 0)
    def _():
        stf_ref[...] = jnp.zeros_like(stf_ref)
        stb_ref[...] = jnp.zeros_like(stb_ref)

    n = HG_SUB
    nsub = qf_ref.shape[1] // n
    nchunk = n // HG_CHUNK
    tri = {rev: _chunk_mask(n, rev) for rev in (False, True)}

    groups = []
    for h in range(HG_HEADS):
        cols = slice(h * HG_DK, (h + 1) * HG_DK)
        for s in range(nsub):
            groups.append(((qf_ref, vf_ref, gf_ref, of_ref), False, h, slice(s * n, (s + 1) * n), cols))
        for s in range(nsub - 1, -1, -1):
            groups.append(((qb_ref, vb_ref, gb_ref, ob_ref), True, h, slice(s * n, (s + 1) * n), cols))

    gs = [refs[2][0, rs, cs] for refs, _, _, rs, cs in groups]
    pos = lax.broadcasted_iota(jnp.int32, (n, HG_DK), 0) & (HG_CHUNK - 1)
    bs = []
    for g, (_, rev, _, _, _) in zip(gs, groups):
        b = g
        step = 1
        while step < HG_CHUNK:
            if rev:
                b = b + jnp.where(pos < HG_CHUNK - step, pltpu.roll(b, n - step, axis=0), 0.0)
            else:
                b = b + jnp.where(pos >= step, pltpu.roll(b, step, axis=0), 0.0)
            step *= 2
        bs.append(b)

    qds, kds, kcs, decs = [], [], [], []
    for g, b, (refs, rev, _, rs, cs) in zip(gs, bs, groups):
        b3 = b.reshape(nchunk, HG_CHUNK, HG_DK)
        tot = b3[:, 0:1, :] if rev else b3[:, HG_CHUNK - 1:HG_CHUNK, :]
        totf = jnp.broadcast_to(tot, b3.shape).reshape(n, HG_DK)
        k = 1.0 - jnp.exp(g)
        qds.append((refs[0][0, rs, cs] * jnp.exp(b)).astype(BF16))
        kds.append((k * jnp.exp(-b)).astype(BF16))
        kcs.append((k * jnp.exp(totf - b)).astype(BF16))
        decs.append(jnp.exp(tot))

    amats = [_dot_nt(qd, kd) for qd, kd in zip(qds, kds)]
    vs = [refs[1][0, rs, cs] for refs, _, _, rs, cs in groups]
    intras = [_dot(jnp.where(tri[rev], a, 0.0).astype(BF16), v)
              for a, v, (_, rev, _, _, _) in zip(amats, vs, groups)]
    chunks = [slice(c * HG_CHUNK, (c + 1) * HG_CHUNK) for c in range(nchunk)]
    uts = [[_dot_tn(v[sl], kc[sl]) for sl in chunks] for v, kc in zip(vs, kcs)]

    chains = {}
    for gi, (_, rev, h, _, _) in enumerate(groups):
        order = range(nchunk - 1, -1, -1) if rev else range(nchunk)
        chains.setdefault((rev, h), []).extend((gi, ci) for ci in order)
    states = {key: (stb_ref if key[0] else stf_ref)[key[1]] for key in chains}
    outs = [[None] * nchunk for _ in groups]
    for t in range(nsub * nchunk):
        for key, steps in chains.items():
            gi, ci = steps[t]
            st = states[key]
            outs[gi][ci] = intras[gi][chunks[ci]] + _dot_nt(qds[gi][chunks[ci]], st.astype(BF16))
            states[key] = st * decs[gi][ci] + uts[gi][ci]
    for gi, (refs, _, _, rs, cs) in enumerate(groups):
        refs[3][0, rs, cs] = jnp.concatenate(outs[gi], axis=0)
    for (rev, h), st in states.items():
        (stb_ref if rev else stf_ref)[h] = st


def _hgrn(hq, hv, gf, gb):
    B, L, _ = hq.shape
    rows = min(HG_ROWS, L)
    nb = L // rows
    fwd = pl.BlockSpec((1, rows, HG_W), lambda b, i: (b, i, 0))
    bwd = pl.BlockSpec((1, rows, HG_W), lambda b, i: (b, nb - 1 - i, 0))
    o = jax.ShapeDtypeStruct((B, L, HG_W), F32)
    state = pltpu.VMEM((HG_HEADS, HG_DK, HG_DK), F32)
    return pl.pallas_call(
        _hgrn_kernel,
        grid=(B, nb),
        in_specs=[fwd, fwd, fwd, bwd, bwd, bwd],
        out_specs=(fwd, bwd),
        out_shape=(o, o),
        scratch_shapes=[state, state],
        compiler_params=pltpu.CompilerParams(
            dimension_semantics=("arbitrary", "arbitrary"), vmem_limit_bytes=VMEM_LIMIT),
        name="hgrn2",
    )(hq, hv, gf, hq, hv, gb)


def _attn_kernel(flag_ref, inv_ref, q_ref, k_ref, vt_ref, o_ref, acc_ref, *, tk, inner):
    tq = q_ref.shape[2]
    nk = k_ref.shape[2] // tk
    qts = [q_ref[0, j * QK_DEPTH:(j + 1) * QK_DEPTH, :] for j in range(AT_GROUP)]
    inv = inv_ref[0]

    def scores(blk, j, stabilised):
        start = pl.multiple_of(blk * tk, tk)
        st = _dot(k_ref[0, 0, pl.ds(start, tk), :], qts[j]) * inv
        return st if stabilised else jnp.exp2(st).astype(BF16)

    def run(stabilised, lookahead):
        def body(it, carry):
            pending, state = carry
            pending = list(pending)
            state = list(state)
            for n in range(inner * AT_GROUP):
                blk = it * inner + n // AT_GROUP
                j = n % AT_GROUP
                start = pl.multiple_of(blk * tk, tk)
                vb = vt_ref[0, 0, :, pl.ds(start, tk)]
                st = pending.pop(0)
                ahead = n + lookahead
                pending.append(scores(jnp.minimum(it * inner + ahead // AT_GROUP, nk - 1), ahead % AT_GROUP,
                                      stabilised))
                if stabilised:
                    m, acc = state[j]
                    m_new = jnp.maximum(m, jnp.max(st, axis=0, keepdims=True))
                    p = jnp.exp2(st - m_new).astype(BF16)
                    state[j] = (m_new, jnp.exp2(m - m_new) * acc + _dot(vb, p))
                else:
                    state[j] = state[j] + _dot(vb, st)
            return tuple(pending), tuple(state)

        zero = jnp.zeros((VT_ROWS, tq), F32)
        init = tuple((jnp.full((1, tq), -1e30, F32), zero) if stabilised else zero for _ in range(AT_GROUP))
        first = tuple(scores(min(n // AT_GROUP, nk - 1), n % AT_GROUP, stabilised) for n in range(lookahead))
        _, final = lax.fori_loop(0, nk // inner, body, (first, init))
        for j in range(AT_GROUP):
            acc_ref[j] = final[j][1] if stabilised else final[j]

    @pl.when(flag_ref[0] != 0)
    def _():
        run(False, LOOKAHEAD_BOUNDED)

    @pl.when(flag_ref[0] == 0)
    def _():
        run(True, LOOKAHEAD)

    for j in range(AT_GROUP):
        o_ref[0, j * AT_HD:(j + 1) * AT_HD, :] = (
            acc_ref[j, 0:AT_HD, :] / acc_ref[j, AT_HD:AT_HD + 1, :]).astype(o_ref.dtype)


def _attention(flag, inv, aq, ak, vt):
    B, _, L = aq.shape
    tq = min(TQ, L)
    tk = min(TK, L)
    gw = AT_GROUP * AT_HD
    return pl.pallas_call(
        functools.partial(_attn_kernel, tk=tk, inner=min(KV_BLOCKS_PER_ITER, L // tk)),
        grid=(B, AT_KV_HEADS, L // tq),
        in_specs=[
            pl.BlockSpec(memory_space=pltpu.SMEM),
            pl.BlockSpec(memory_space=pltpu.SMEM),
            pl.BlockSpec((1, AT_GROUP * QK_DEPTH, tq), lambda b, g, i: (b, g, i)),
            pl.BlockSpec((1, 1, L, QK_DEPTH), lambda b, g, i: (b, g, 0, 0)),
            pl.BlockSpec((1, 1, VT_ROWS, L), lambda b, g, i: (b, g, 0, 0)),
        ],
        out_specs=pl.BlockSpec((1, gw, tq), lambda b, g, i: (b, g, i)),
        out_shape=jax.ShapeDtypeStruct((B, AT_W, L), BF16),
        scratch_shapes=[pltpu.VMEM((AT_GROUP, VT_ROWS, tq), F32)],
        compiler_params=pltpu.CompilerParams(
            dimension_semantics=("arbitrary", "arbitrary", "arbitrary"), vmem_limit_bytes=VMEM_LIMIT),
        name="attention",
    )(flag, inv, aq, ak, vt)


def _merge_kernel(x_ref, of_ref, ob_ref, oat_ref, n1_ref, wgh_ref, wgab_ref, ong_ref, wa_ref, wb_ref,
                  wo_ref, n2_ref, *rest, ncast):
    cast_src, (h_ref, hn_ref), cast_dst = rest[:ncast], rest[ncast:ncast + 2], rest[ncast + 2:]
    _cast_blocks(cast_src, cast_dst)
    tm = x_ref.shape[1]
    for part in range(MERGE_SPLIT):
        rows = slice(part * tm // MERGE_SPLIT, (part + 1) * tm // MERGE_SPLIT)
        x = x_ref[0, rows, :]
        u = _rms_rows(x, n1_ref[...]).astype(BF16)
        hgate = _dot(u, wgh_ref[...])
        yb = _dot_tn(oat_ref[0, :, rows], wb_ref[...])
        gb = _dot(u, wgab_ref[:, D_MODEL:2 * D_MODEL])
        o = of_ref[0, rows, :] + ob_ref[0, rows, :]
        heads = []
        for h in range(HG_HEADS):
            sl = slice(h * HG_DK, (h + 1) * HG_DK)
            heads.append(_rms_rows(o[:, sl], ong_ref[...]))
        oa = (jnp.concatenate(heads, axis=1) * (hgate * _sigmoid(hgate))).astype(BF16)
        ga = _dot(u, wgab_ref[:, 0:D_MODEL])
        gated_b = _sigmoid(gb) * yb
        ya = _dot(oa, wa_ref[...])
        merged = (_sigmoid(ga) * ya + gated_b).astype(BF16)
        h = x + _dot(merged, wo_ref[...])
        h_ref[0, rows, :] = h
        hn_ref[0, rows, :] = _rms_rows(h, n2_ref[...]).astype(BF16)


def _merge(x, o_f, o_b, o_at, n1, wgh, wgab, ong, wa, wb, wo, n2, later_weights):
    B, L, D = x.shape
    tm = min(TM_MERGE, L)
    cast_in, cast_out, cast_types, cast_arrays = _cast_plan(later_weights, B, L // tm)
    tok = lambda w: pl.BlockSpec((1, tm, w), lambda b, i: (b, i, 0))
    return pl.pallas_call(
        functools.partial(_merge_kernel, ncast=len(later_weights)),
        grid=(B, L // tm),
        in_specs=[tok(D), tok(HG_W), tok(HG_W), pl.BlockSpec((1, AT_W, tm), lambda b, i: (b, 0, i)),
                  _const_spec(n1.shape), _const_spec(wgh.shape),
                  _const_spec(wgab.shape), _const_spec(ong.shape), _const_spec(wa.shape),
                  _const_spec(wb.shape), _const_spec(wo.shape), _const_spec(n2.shape)] + cast_in,
        out_specs=(tok(D), tok(D)) + tuple(cast_out),
        out_shape=(jax.ShapeDtypeStruct((B, L, D), F32), jax.ShapeDtypeStruct((B, L, D), BF16))
        + tuple(cast_types),
        compiler_params=pltpu.CompilerParams(
            dimension_semantics=("arbitrary", "arbitrary"), vmem_limit_bytes=VMEM_LIMIT),
        name="merge",
    )(x, o_f, o_b, o_at, n1, wgh, wgab, ong, wa, wb, wo, n2, *cast_arrays)


def _ffn_kernel(h_ref, hn_ref, prev_ref, next_ref, wup_ref, cw_ref, cb_ref, wdn_ref, y_ref, ext_ref,
                act_ref):
    i = pl.program_id(1)
    nt = pl.num_programs(1)
    tm = hn_ref.shape[1]
    ext = tm + 2 * HALO
    ext_ref[0:HALO, :] = jnp.where(i > 0, prev_ref[0], jnp.zeros_like(prev_ref[0]))
    ext_ref[HALO:HALO + tm, :] = hn_ref[0]
    ext_ref[HALO + tm:ext, :] = jnp.where(i < nt - 1, next_ref[0], jnp.zeros_like(next_ref[0]))
    hx = ext_ref[...]

    def conv(t, off):
        w = cw_ref[:, off:off + FF_CHUNK]
        before = pltpu.roll(t, 1, axis=0)[HALO:HALO + tm]
        after = pltpu.roll(t, ext - 1, axis=0)[HALO:HALO + tm]
        return (before * w[0:1] + t[HALO:HALO + tm] * w[1:2] + after * w[2:3]
                + cb_ref[:, off:off + FF_CHUNK])

    def up(j):
        off = j * FF_CHUNK
        return (_dot(hx, wup_ref[:, off:off + FF_CHUNK]),
                _dot(hx, wup_ref[:, D_FF + off:D_FF + off + FF_CHUNK]))

    nchunk = D_FF // FF_CHUNK
    bounds = [round(g * nchunk / FF_DOWN_GROUPS) * FF_CHUNK for g in range(FF_DOWN_GROUPS + 1)]
    y = h_ref[0]
    nxt = up(0)
    for j in range(nchunk):
        off = j * FF_CHUNK
        upv, upg = nxt
        if j + 1 < nchunk:
            nxt = up(j + 1)
        val = conv(upv, off)
        gate = conv(upg, D_FF + off)
        act_ref[:, off:off + FF_CHUNK] = (gate * _sigmoid(gate) * val).astype(BF16)
        if off + FF_CHUNK in bounds[1:]:
            lo = bounds[bounds.index(off + FF_CHUNK) - 1]
            y = y + _dot(act_ref[:, lo:off + FF_CHUNK], wdn_ref[lo:off + FF_CHUNK, :])
    y_ref[0] = y


def _ffn(h, hn, wup, cw, cb, wdn):
    B, L, D = h.shape
    tm = min(TM_FFN, L)
    per = tm // HALO
    nh = L // HALO
    tok = pl.BlockSpec((1, tm, D), lambda b, i: (b, i, 0))
    return pl.pallas_call(
        _ffn_kernel,
        grid=(B, L // tm),
        in_specs=[
            tok, tok,
            pl.BlockSpec((1, HALO, D), lambda b, i: (b, jnp.maximum(i * per - 1, 0), 0)),
            pl.BlockSpec((1, HALO, D), lambda b, i: (b, jnp.minimum((i + 1) * per, nh - 1), 0)),
            _const_spec(wup.shape), _const_spec(cw.shape), _const_spec(cb.shape), _const_spec(wdn.shape),
        ],
        out_specs=tok,
        out_shape=jax.ShapeDtypeStruct((B, L, D), F32),
        scratch_shapes=[pltpu.VMEM((tm + 2 * HALO, D), BF16), pltpu.VMEM((tm, D_FF), BF16)],
        compiler_params=pltpu.CompilerParams(
            dimension_semantics=("arbitrary", "arbitrary"), vmem_limit_bytes=VMEM_LIMIT),
        name="convffn",
    )(h, hn, hn, hn, wup, cw, cb, wdn)


def _rope_tables(L):
    pos = np.arange(L)
    inv = ROPE_THETA ** (-np.arange(0, ROPE_AXIS_DIM, 2, dtype=np.float64) / ROPE_AXIS_DIM)
    ang_r = (pos // GRID_W)[:, None] * inv[None, :]
    ang_c = (pos % GRID_W)[:, None] * inv[None, :]
    ang = np.concatenate([ang_r, ang_r, ang_c, ang_c] * (LANES // AT_HD), axis=-1)
    return jnp.asarray(np.cos(ang), dtype=F32), jnp.asarray(np.sin(ang), dtype=F32)


def _layer(x, norm1_g, w_in, lb_f, lb_b, onorm_g, q_norm_g, k_norm_g, w_a, w_b, w_out, norm2_g,
           w_up, conv_w, conv_b, w_down):
    B, L, D = x.shape
    c_hg = 4 * HG_W
    c_gate = 5 * HG_W
    c_at = c_gate + AT_W + 2 * AT_KV_W
    merge_weights = ((w_in, c_hg, c_gate - c_hg), (w_in, c_at, w_in.shape[1] - c_at),
                     w_a, w_b, w_out)
    n1 = norm1_g.reshape(1, D)
    n2 = norm2_g.reshape(1, D)
    qg2 = jnp.tile(q_norm_g.reshape(1, AT_HD), (1, LANES // AT_HD))
    kg2 = jnp.tile(k_norm_g.reshape(1, AT_HD), (1, LANES // AT_HD))
    cos, sin = _rope_tables(L)
    blk = np.arange(AT_W) // AT_HD
    gsum = jnp.asarray((blk[:, None] == blk[None, :]).astype(np.float32) / AT_HD, dtype=BF16)

    q_bound = LOG2_E * jnp.max(jnp.abs(q_norm_g))
    k_bound = AT_HD ** 0.5 * jnp.max(jnp.abs(k_norm_g))
    score_bound = 1.01 * q_bound * k_bound
    flag = (score_bound <= SCORE_LIMIT).astype(jnp.int32).reshape(1)

    def f8_shift(bound):
        return jnp.clip(jnp.floor(jnp.log2(F8_TARGET_MAX / jnp.maximum(bound, 1e-30))), -60.0, 60.0)

    eq, ek = f8_shift(q_bound), f8_shift(k_bound)
    qs2 = jnp.full((1, LANES), AT_HD ** -0.5 * LOG2_E, F32) * jnp.exp2(eq)
    ks2 = jnp.full((1, LANES), 1.0, F32) * jnp.exp2(ek)
    inv = jnp.exp2(-(eq + ek)).astype(F32).reshape(1)

    hq, hv, gf, gb, aq, ak, vt, wgh, wgab, wa, wb, wo = _inproj(
        x, n1, w_in, (0, c_hg), (c_gate, c_at - c_gate), lb_f, lb_b, qg2, kg2, qs2, ks2, cos, sin, gsum,
        merge_weights)
    o_f, o_b = _hgrn(hq, hv, gf, gb)
    o_at = _attention(flag, inv, aq, ak, vt)
    h, hn, wup, wdn = _merge(x, o_f, o_b, o_at, n1, wgh, wgab, onorm_g.reshape(1, HG_DK), wa, wb, wo, n2,
                             (w_up, w_down))
    return _ffn(h, hn, wup, conv_w, conv_b.reshape(1, 2 * D_FF), wdn)


def kernel(x, norm1_g, w_in, hg_lb_fwd, hg_lb_bwd, hg_onorm_g, q_norm_g, k_norm_g, w_branch_a,
           w_branch_b, w_out, norm2_g, w_up, conv_w, conv_b, w_down):
    depth = w_in.shape[0]
    assert depth == 1 and hg_lb_fwd.shape[0] == 2
    l = 0
    return _layer(x, norm1_g[l], w_in[l], hg_lb_fwd, hg_lb_bwd, hg_onorm_g[l], q_norm_g[l], k_norm_g[l],
                  w_branch_a[l], w_branch_b[l], w_out[l], norm2_g[l], w_up[l], conv_w[l], conv_b[l],
                  w_down[l])
```
